```python
import math
import jax, jax.numpy as jnp
from jax import lax
import numpy as np

D_MODEL = 1024
BATCH = 8
SEQ = 8192
DEPTH = 4

CHUNK = 64
N_MIXERS = 3
EPS = 1e-6
GLA_HEADS = 4
GLA_DK = D_MODEL // 2
GLA_DV = D_MODEL
GLA_GATE_RANK = 16
GLA_GATE_NORM = 16.0
MLSTM_HEADS = 4
MLSTM_INNER = 2 * D_MODEL
MLSTM_CONV = 4
MLSTM_QK_BLOCK = 4
S5_GROUP = 16
S5_GROUPS = D_MODEL // S5_GROUP
S5_STATE = 64
N_EXPERTS = 32
TOP_K = 4
D_FF = D_MODEL
SWIGLU_LIMIT = 7.0
SWIGLU_ALPHA = 1.702
N_LAYERS_A = (DEPTH + 2) // 3
N_LAYERS_B = (DEPTH + 1) // 3
N_LAYERS_C = DEPTH // 3
N_KEYS = 64

kernel_name = "chunk_causal_hybrid_gla_mlstm_s5_moe"


def _rmsnorm(x, g):
    x32 = x.astype(jnp.float32)
    y = x32 * lax.rsqrt(jnp.mean(x32 * x32, axis=-1, keepdims=True) + EPS)
    return (y * g.astype(jnp.float32)).astype(x.dtype)


def _to_chunks(t):
    b, l, h, d = t.shape
    return t.reshape(b, l // CHUNK, CHUNK, h, d).transpose(1, 0, 3, 2, 4)


def _from_chunks(t):
    nc, b, h, c, d = t.shape
    return t.transpose(1, 0, 3, 2, 4).reshape(b, nc * c, h * d)


def gla_mixer(h, w_in, w_gk1, w_gk2, b_gk, g_onorm, w_out):
    b, l, _ = h.shape
    dk = GLA_DK // GLA_HEADS
    dv = GLA_DV // GLA_HEADS
    proj = (h @ w_in).astype(jnp.float32)
    q, k, v, g = jnp.split(proj, [GLA_DK, 2 * GLA_DK, 2 * GLA_DK + GLA_DV], axis=-1)
    gk = ((h @ w_gk1) @ w_gk2 + b_gk).astype(jnp.float32)
    log_a = jax.nn.log_sigmoid(gk) / GLA_GATE_NORM
    heads = lambda t: _to_chunks(t.reshape(b, l, GLA_HEADS, -1))

    def step(state, inp):
        qc, kc, vc, ac = inp
        cum = jnp.cumsum(ac, axis=2)
        tot = cum[:, :, -1:, :]
        k_dec = kc * jnp.exp(tot - cum)
        state = jnp.exp(tot)[:, :, 0, :, None] * state + jnp.einsum('bhck,bhcv->bhkv', k_dec, vc)
        return state, jnp.einsum('bhck,bhkv->bhcv', qc, state)

    s0 = jnp.zeros((b, GLA_HEADS, dk, dv), jnp.float32)
    _, o = lax.scan(step, s0, (heads(q * dk ** -0.5), heads(k), heads(v), heads(log_a)))
    o = o * lax.rsqrt(jnp.mean(o * o, axis=-1, keepdims=True) + EPS) * g_onorm.astype(jnp.float32)
    o = _from_chunks(o) * jax.nn.silu(g)
    return o.astype(h.dtype) @ w_out


def _blockdiag(t, w):
    b, l, _ = t.shape
    nb, bs, _ = w.shape
    return jnp.einsum('blni,nio->blno', t.reshape(b, l, nb, bs), w).reshape(b, l, nb * bs)


def mlstm_mixer(h, w_up, conv_w, conv_b, w_q, w_k, w_v, w_if, b_if, skip, g_norm, w_down):
    b, l, _ = h.shape
    dh = MLSTM_INNER // MLSTM_HEADS
    xm, z = jnp.split(h @ w_up, 2, axis=-1)
    xc = lax.conv_general_dilated(xm, conv_w[:, None, :], window_strides=(1,),
                                  padding=[(MLSTM_CONV - 1, 0)],
                                  dimension_numbers=('NWC', 'WIO', 'NWC'),
                                  feature_group_count=MLSTM_INNER) + conv_b
    xc = jax.nn.silu(xc)
    q = _blockdiag(xc, w_q).astype(jnp.float32)
    k = _blockdiag(xc, w_k).astype(jnp.float32)
    v = _blockdiag(xm, w_v).astype(jnp.float32)
    gates = (jnp.concatenate([q, k, v], axis=-1) @ w_if + b_if).astype(jnp.float32)
    i_pre, f_pre = jnp.split(gates, 2, axis=-1)
    log_f = jax.nn.log_sigmoid(f_pre)
    heads = lambda t: _to_chunks(t.reshape(b, l, MLSTM_HEADS, -1))
    gate_chunks = lambda t: _to_chunks(t[..., None])[..., 0]

    def step(carry, inp):
        cmat, nvec, m = carry
        qc, kc, vc, ic, fc = inp
        cum = jnp.cumsum(fc, axis=-1)
        tot = cum[..., -1]
        logw = tot[..., None] - cum + ic
        m_new = jnp.maximum(tot + m, jnp.max(logw, axis=-1))
        decay = jnp.exp(tot + m - m_new)
        kw = kc * jnp.exp(logw - m_new[..., None])[..., None]
        cmat = decay[..., None, None] * cmat + jnp.einsum('bhck,bhcv->bhkv', kw, vc)
        nvec = decay[..., None] * nvec + jnp.sum(kw, axis=2)
        num = jnp.einsum('bhck,bhkv->bhcv', qc, cmat)
        den = jnp.maximum(jnp.abs(jnp.einsum('bhck,bhk->bhc', qc, nvec)),
                          jnp.exp(-m_new)[..., None])
        return (cmat, nvec, m_new), num / den[..., None]

    carry0 = (jnp.zeros((b, MLSTM_HEADS, dh, dh), jnp.float32),
              jnp.zeros((b, MLSTM_HEADS, dh), jnp.float32),
              jnp.zeros((b, MLSTM_HEADS), jnp.float32))
    _, hc = lax.scan(step, carry0, (heads(q), heads(k * dh ** -0.5), heads(v),
                                    gate_chunks(i_pre), gate_chunks(log_f)))
    mu = jnp.mean(hc, axis=-1, keepdims=True)
    var = jnp.var(hc, axis=-1, keepdims=True)
    hn = _from_chunks((hc - mu) * lax.rsqrt(var + EPS)) * g_norm.astype(jnp.float32)
    out = (hn + skip.astype(jnp.float32) * xc.astype(jnp.float32)) * jax.nn.silu(z.astype(jnp.float32))
    return out.astype(h.dtype) @ w_down


def s5_mixer(h, w_in, a_re, a_im, log_dt, b_re, b_im, c_re, c_im, d_skip, w_out):
    b, l, _ = h.shape
    u = (h @ w_in).astype(jnp.float32).reshape(b, l, S5_GROUPS, S5_GROUP)
    a = lax.complex(a_re.astype(jnp.float32), a_im.astype(jnp.float32))
    dt = jnp.exp(log_dt.astype(jnp.float32))[:, None]
    a_bar = jnp.exp(a * dt)
    b_mat = lax.complex(b_re.astype(jnp.float32), b_im.astype(jnp.float32))
    b_bar = ((a_bar - 1.0) / a)[..., None] * b_mat
    bu = jnp.einsum('blgc,gpc->blgp', u.astype(jnp.complex64), b_bar)
    a_seq = jnp.broadcast_to(a_bar, (1, l) + a_bar.shape)

    def combine(e1, e2):
        a1, s1 = e1
        a2, s2 = e2
        return a1 * a2, a2 * s1 + s2

    _, states = lax.associative_scan(combine, (a_seq, bu), axis=1)
    c_mat = lax.complex(c_re.astype(jnp.float32), c_im.astype(jnp.float32))
    y = jnp.einsum('blgp,gcp->blgc', states, c_mat).real \
        + d_skip.astype(jnp.float32).reshape(S5_GROUPS, S5_GROUP) * u
    y = jax.nn.gelu(y.reshape(b, l, D_MODEL)).astype(h.dtype)
    glu = y @ w_out
    return glu[..., :D_MODEL] * jax.nn.sigmoid(glu[..., D_MODEL:])


def moe_ffn(h, w_router, b_router, w_gu, b_gu, w_down, b_down):
    b, l, d = h.shape
    t = h.reshape(-1, d)
    logits = (t @ w_router).astype(jnp.float32) + b_router.astype(jnp.float32)
    top_v, top_i = lax.top_k(logits, TOP_K)
    top_w = jax.nn.softmax(top_v, axis=-1)
    comb = jnp.einsum('tk,tke->te', top_w,
                      jax.nn.one_hot(top_i, N_EXPERTS, dtype=jnp.float32)).astype(h.dtype)
    out = jnp.zeros_like(t)
    for e in range(N_EXPERTS):
        gu = t @ w_gu[e] + b_gu[e]
        gate = jnp.minimum(gu[:, :D_FF], SWIGLU_LIMIT)
        up = jnp.clip(gu[:, D_FF:], -SWIGLU_LIMIT, SWIGLU_LIMIT)
        act = (up + 1.0) * gate * jax.nn.sigmoid(SWIGLU_ALPHA * gate)
        out = out + comb[:, e:e + 1] * (act @ w_down[e] + b_down[e])
    return out.reshape(b, l, d)


def setup_inputs(seed: int = 0) -> dict:
    key = jax.random.key(seed)
    ks = iter(jax.random.split(key, N_KEYS))
    f32 = jnp.float32
    nrm = lambda shape, std: std * jax.random.normal(next(ks), shape, f32)
    gain = lambda shape: 1.0 + nrm(shape, 0.02)
    D, NA, NB, NC, H = D_MODEL, N_LAYERS_A, N_LAYERS_B, N_LAYERS_C, MLSTM_HEADS
    inp = {}
    inp['x'] = nrm((BATCH, SEQ, D), 1.0)
    inp['c'] = nrm((BATCH, D), 1.0)
    inp['g_mix'] = gain((DEPTH, D))
    inp['g_ffn'] = gain((DEPTH, D))
    inp['w_ada'] = nrm((DEPTH, D, 6 * D), 0.01)
    inp['b_ada'] = nrm((DEPTH, 6 * D), 0.02)
    inp['gla_w_in'] = nrm((NA, D, 2 * GLA_DK + 2 * GLA_DV), D ** -0.5)
    inp['gla_w_gk1'] = nrm((NA, D, GLA_GATE_RANK), D ** -0.5)
    inp['gla_w_gk2'] = nrm((NA, GLA_GATE_RANK, GLA_DK), GLA_GATE_RANK ** -0.5)
    inp['gla_b_gk'] = nrm((NA, GLA_DK), 0.1)
    inp['gla_g_onorm'] = gain((NA, GLA_DV // GLA_HEADS))
    inp['gla_w_out'] = nrm((NA, GLA_DV, D), GLA_DV ** -0.5)
    nblk = MLSTM_INNER // MLSTM_QK_BLOCK
    inp['ml_w_up'] = nrm((NB, D, 2 * MLSTM_INNER), D ** -0.5)
    inp['ml_conv_w'] = nrm((NB, MLSTM_CONV, MLSTM_INNER), MLSTM_CONV ** -0.5)
    inp['ml_conv_b'] = nrm((NB, MLSTM_INNER), 0.01)
    inp['ml_w_q'] = nrm((NB, nblk, MLSTM_QK_BLOCK, MLSTM_QK_BLOCK), MLSTM_QK_BLOCK ** -0.5)
    inp['ml_w_k'] = nrm((NB, nblk, MLSTM_QK_BLOCK, MLSTM_QK_BLOCK), MLSTM_QK_BLOCK ** -0.5)
    inp['ml_w_v'] = nrm((NB, nblk, MLSTM_QK_BLOCK, MLSTM_QK_BLOCK), MLSTM_QK_BLOCK ** -0.5)
    inp['ml_w_if'] = nrm((NB, 3 * MLSTM_INNER, 2 * H), 0.1 * (3 * MLSTM_INNER) ** -0.5)
    f_bias = jnp.broadcast_to(jnp.linspace(3.0, 6.0, H, dtype=f32), (NB, H))
    inp['ml_b_if'] = jnp.concatenate([nrm((NB, H), 0.1), f_bias + nrm((NB, H), 0.1)], axis=-1)
    inp['ml_skip'] = gain((NB, MLSTM_INNER))
    inp['ml_g_norm'] = gain((NB, MLSTM_INNER))
    inp['ml_w_down'] = nrm((NB, MLSTM_INNER, D), MLSTM_INNER ** -0.5)
    G, P = S5_GROUPS, S5_STATE
    inp['s5_w_in'] = nrm((NC, D, D), D ** -0.5)
    inp['s5_a_re'] = -0.5 + nrm((NC, G, P), 0.01)
    inp['s5_a_im'] = math.pi * jnp.broadcast_to(jnp.arange(P, dtype=f32), (NC, G, P)) + nrm((NC, G, P), 0.01)
    inp['s5_log_dt'] = jax.random.uniform(next(ks), (NC, G), f32, math.log(0.001), math.log(0.1))
    inp['s5_b_re'] = nrm((NC, G, P, S5_GROUP), (2.0 * S5_GROUP) ** -0.5)
    inp['s5_b_im'] = nrm((NC, G, P, S5_GROUP), (2.0 * S5_GROUP) ** -0.5)
    inp['s5_c_re'] = nrm((NC, G, S5_GROUP, P), (2.0 * P) ** -0.5)
    inp['s5_c_im'] = nrm((NC, G, S5_GROUP, P), (2.0 * P) ** -0.5)
    inp['s5_d'] = nrm((NC, D), 1.0)
    inp['s5_w_out'] = nrm((NC, D, 2 * D), D ** -0.5)
    inp['moe_w_router'] = nrm((DEPTH, D, N_EXPERTS), D ** -0.5)
    inp['moe_b_router'] = nrm((DEPTH, N_EXPERTS), 0.01)
    inp['moe_w_gu'] = nrm((DEPTH, N_EXPERTS, D, 2 * D_FF), D ** -0.5)
    inp['moe_b_gu'] = nrm((DEPTH, N_EXPERTS, 2 * D_FF), 0.01)
    inp['moe_w_down'] = nrm((DEPTH, N_EXPERTS, D_FF, D), D_FF ** -0.5)
    inp['moe_b_down'] = nrm((DEPTH, N_EXPERTS, D), 0.01)
    inp['g_final'] = gain((D,))
    return inp


def reference(x, c, g_mix, g_ffn, w_ada, b_ada,
              gla_w_in, gla_w_gk1, gla_w_gk2, gla_b_gk, gla_g_onorm, gla_w_out,
              ml_w_up, ml_conv_w, ml_conv_b, ml_w_q, ml_w_k, ml_w_v, ml_w_if, ml_b_if,
              ml_skip, ml_g_norm, ml_w_down,
              s5_w_in, s5_a_re, s5_a_im, s5_log_dt, s5_b_re, s5_b_im, s5_c_re, s5_c_im,
              s5_d, s5_w_out,
              moe_w_router, moe_b_router, moe_w_gu, moe_b_gu, moe_w_down, moe_b_down,
              g_final):
    c_act = jax.nn.silu(c)
    for i in range(DEPTH):
        mod = (c_act @ w_ada[i] + b_ada[i])[:, None, :]
        sh1, sc1, gt1, sh2, sc2, gt2 = jnp.split(mod, 6, axis=-1)
        h = _rmsnorm(x, g_mix[i]) * (1.0 + sc1) + sh1
        kind, j = i % N_MIXERS, i // N_MIXERS
        if kind == 0:
            y = gla_mixer(h, gla_w_in[j], gla_w_gk1[j], gla_w_gk2[j], gla_b_gk[j],
                          gla_g_onorm[j], gla_w_out[j])
        elif kind == 1:
            y = mlstm_mixer(h, ml_w_up[j], ml_conv_w[j], ml_conv_b[j], ml_w_q[j], ml_w_k[j],
                            ml_w_v[j], ml_w_if[j], ml_b_if[j], ml_skip[j], ml_g_norm[j],
                            ml_w_down[j])
        else:
            y = s5_mixer(h, s5_w_in[j], s5_a_re[j], s5_a_im[j], s5_log_dt[j], s5_b_re[j],
                         s5_b_im[j], s5_c_re[j], s5_c_im[j], s5_d[j], s5_w_out[j])
        x = x + gt1 * y
        h = _rmsnorm(x, g_ffn[i]) * (1.0 + sc2) + sh2
        x = x + gt2 * moe_ffn(h, moe_w_router[i], moe_b_router[i], moe_w_gu[i], moe_b_gu[i],
                              moe_w_down[i], moe_b_down[i])
    return _rmsnorm(x, g_final)
```

```python
import functools

import jax
import jax.numpy as jnp
from jax import lax
from jax.experimental import pallas as pl
from jax.experimental.pallas import tpu as pltpu

F32 = jnp.float32
BF16 = jnp.bfloat16
HIGHEST = lax.Precision.HIGHEST

EPS = 1e-6
CHUNK = 64
N_MIXERS = 3
GLA_HEADS = 4
GLA_GATE_NORM = 16.0
MLSTM_HEADS = 4
S5_GROUP = 16
TOP_K = 4
SWIGLU_LIMIT = 7.0
SWIGLU_ALPHA = 1.702

LANES = 128
SUBLANES = 8
MXU = 256
VMEM_LIMIT_BYTES = 48 * 1024 * 1024

NEG_BIG = -1e30


def _tiles(seq, tokens, n_experts):
    row = min(512, seq)
    gla = min(512, seq)
    mlstm = min(256, seq)
    s5 = min(32, seq)
    per_expert = tokens * TOP_K // n_experts
    expert = 512 if per_expert >= 2048 else 128
    return dict(row=row, gla=gla, mlstm=mlstm, s5=s5, expert=expert)


def _params(sem):
    return pltpu.CompilerParams(dimension_semantics=sem, vmem_limit_bytes=VMEM_LIMIT_BYTES)


def _normmod(x, g, sc, sh):
    y = x * lax.rsqrt(jnp.mean(x * x, axis=-1, keepdims=True) + EPS)
    return (y * g) * (1.0 + sc) + sh


def _log_sigmoid(x):
    return jnp.minimum(x, 0.0) - jnp.log(1.0 + jnp.exp(-jnp.abs(x)))


def _silu(x):
    return x * jax.nn.sigmoid(x)


def _mod_spec(d, j):
    return pl.BlockSpec((None, None, 1, d), lambda b, t: (b, j, 0, 0))


def _row_spec(tm, n):
    return pl.BlockSpec((None, tm, n), lambda b, t: (b, t, 0))


def _const_spec(shape):
    nd = len(shape)
    return pl.BlockSpec(shape, lambda b, t: (0,) * nd)


def _mod_kernel(c_ref, w_ref, b_ref, o_ref):
    ca = _silu(c_ref[...]).astype(BF16)
    o_ref[0] = jnp.dot(ca, w_ref[0].astype(BF16), preferred_element_type=F32) + b_ref[0]


def _modulation(c, w_ada, b_ada):
    depth, d, n = w_ada.shape
    b = c.shape[0]
    tn = n // 4
    return pl.pallas_call(
        _mod_kernel,
        out_shape=jax.ShapeDtypeStruct((depth, b, n), F32),
        grid=(depth, n // tn),
        in_specs=[pl.BlockSpec((b, d), lambda i, j: (0, 0)),
                  pl.BlockSpec((1, d, tn), lambda i, j: (i, 0, j)),
                  pl.BlockSpec((1, 1, tn), lambda i, j: (i, 0, j))],
        out_specs=pl.BlockSpec((1, b, tn), lambda i, j: (i, 0, j)),
        compiler_params=_params(("parallel", "parallel")),
        name="adaln_mod",
    )(c, w_ada, b_ada.reshape(depth, 1, n))


def _norm_proj_kernel(x_ref, g_ref, sc_ref, sh_ref, w_ref, o_ref):
    h = _normmod(x_ref[...], g_ref[...], sc_ref[...], sh_ref[...]).astype(BF16)
    o_ref[...] = jnp.dot(h, w_ref[...], preferred_element_type=F32).astype(o_ref.dtype)


def _norm_proj(x, g, mod, jsc, jsh, w, out_dtype, tm):
    b, l, d = x.shape
    n = w.shape[1]
    return pl.pallas_call(
        _norm_proj_kernel,
        out_shape=jax.ShapeDtypeStruct((b, l, n), out_dtype),
        grid=(b, l // tm),
        in_specs=[_row_spec(tm, d), _const_spec((1, d)), _mod_spec(d, jsc), _mod_spec(d, jsh),
                  _const_spec((d, n))],
        out_specs=_row_spec(tm, n),
        compiler_params=_params(("parallel", "parallel")),
        name="norm_proj",
    )(x, g.reshape(1, d), mod, mod, w)


def _proj_res_kernel(y_ref, w_ref, x_ref, gt_ref, o_ref):
    o_ref[...] = x_ref[...] + gt_ref[...] * jnp.dot(y_ref[...], w_ref[...], preferred_element_type=F32)


def _glu_res_kernel(y_ref, w_ref, x_ref, gt_ref, o_ref):
    d = o_ref.shape[-1]
    glu = jnp.dot(y_ref[...], w_ref[...], preferred_element_type=F32)
    o_ref[...] = x_ref[...] + gt_ref[...] * (glu[:, :d] * jax.nn.sigmoid(glu[:, d:]))


def _proj_res(kernel, y, w, x, mod, jgt, tm, name):
    b, l, d = x.shape
    k = y.shape[-1]
    n = w.shape[1]
    return pl.pallas_call(
        kernel,
        out_shape=jax.ShapeDtypeStruct((b, l, d), F32),
        grid=(b, l // tm),
        in_specs=[_row_spec(tm, k), _const_spec((k, n)), _row_spec(tm, d), _mod_spec(d, jgt)],
        out_specs=_row_spec(tm, d),
        compiler_params=_params(("parallel", "parallel")),
        name=name,
    )(y, w, x, mod)


def _gla_proj_kernel(x_ref, g_ref, sc_ref, sh_ref, w_ref, w1_ref, w2_ref, b2_ref, p_ref, la_ref, *, dk_total, qscale):
    h = _normmod(x_ref[...], g_ref[...], sc_ref[...], sh_ref[...]).astype(BF16)
    proj = jnp.dot(h, w_ref[...], preferred_element_type=F32)
    p_ref[:, :dk_total] = (proj[:, :dk_total] * qscale).astype(BF16)
    p_ref[:, dk_total:] = proj[:, dk_total:].astype(BF16)
    t1 = jnp.dot(h, w1_ref[...], preferred_element_type=F32).astype(BF16)
    gk = jnp.dot(t1, w2_ref[...], preferred_element_type=F32) + b2_ref[...]
    la_ref[...] = _log_sigmoid(gk) * (1.0 / GLA_GATE_NORM)


def _gla_scan_kernel(p_ref, la_ref, gon_ref, o_ref, st_ref, *, tc, dk, dv):
    heads = GLA_HEADS
    dk_total, dv_total = heads * dk, heads * dv

    @pl.when(pl.program_id(1) == 0)
    def _():
        st_ref[...] = jnp.zeros_like(st_ref)

    rr = lax.broadcasted_iota(jnp.int32, (CHUNK, CHUNK), 0)
    cc = lax.broadcasted_iota(jnp.int32, (CHUNK, CHUNK), 1)
    tri = (rr >= cc).astype(F32)

    def chunk(j, carry):
        rows = pl.ds(pl.multiple_of(j * CHUNK, CHUNK), CHUNK)
        for h in range(heads):
            ks = slice(h * dk, (h + 1) * dk)
            vs = slice(h * dv, (h + 1) * dv)
            a = la_ref[rows, ks]
            cum = jnp.dot(tri, a, preferred_element_type=F32, precision=HIGHEST)
            tot = cum[CHUNK - 1:CHUNK, :]
            kblk = p_ref[rows, dk_total + h * dk:dk_total + (h + 1) * dk].astype(F32)
            kd = (kblk * jnp.exp(tot - cum)).astype(BF16)
            v = p_ref[rows, 2 * dk_total + h * dv:2 * dk_total + (h + 1) * dv]
            upd = lax.dot_general(v, kd, (((0,), (0,)), ((), ())), preferred_element_type=F32)
            st = jnp.exp(tot) * st_ref[h] + upd
            st_ref[h] = st
            q = p_ref[rows, ks]
            o = lax.dot_general(q, st.astype(BF16), (((1,), (1,)), ((), ())), preferred_element_type=F32)
            o = o * lax.rsqrt(jnp.mean(o * o, axis=-1, keepdims=True) + EPS) * gon_ref[...]
            g = p_ref[rows, 2 * dk_total + dv_total + h * dv:2 * dk_total + dv_total + (h + 1) * dv].astype(F32)
            o_ref[rows, vs] = (o * _silu(g)).astype(BF16)
        return carry

    lax.fori_loop(0, tc // CHUNK, chunk, 0)


def _gla_mixer(x, g_mix, mod, w_in, w_gk1, w_gk2, b_gk, g_onorm, w_out, tiles):
    b, l, d = x.shape
    tm, tc = tiles["row"], tiles["gla"]
    rank = w_gk1.shape[1]
    dk_total = w_gk2.shape[1]
    dv_total = w_out.shape[0]
    dk, dv = dk_total // GLA_HEADS, dv_total // GLA_HEADS
    n = w_in.shape[1]
    w1 = jnp.zeros((d, LANES), BF16).at[:, :rank].set(w_gk1.astype(BF16))
    w2 = jnp.zeros((LANES, dk_total), BF16).at[:rank, :].set(w_gk2.astype(BF16))
    p, la = pl.pallas_call(
        functools.partial(_gla_proj_kernel, dk_total=dk_total, qscale=dk ** -0.5),
        out_shape=(jax.ShapeDtypeStruct((b, l, n), BF16), jax.ShapeDtypeStruct((b, l, dk_total), F32)),
        grid=(b, l // tm),
        in_specs=[_row_spec(tm, d), _const_spec((1, d)), _mod_spec(d, 1), _mod_spec(d, 0),
                  _const_spec((d, n)), _const_spec((d, LANES)), _const_spec((LANES, dk_total)),
                  _const_spec((1, dk_total))],
        out_specs=(_row_spec(tm, n), _row_spec(tm, dk_total)),
        compiler_params=_params(("parallel", "parallel")),
        name="gla_proj",
    )(x, g_mix.reshape(1, d), mod, mod, w_in.astype(BF16), w1, w2, b_gk.reshape(1, dk_total))
    o = pl.pallas_call(
        functools.partial(_gla_scan_kernel, tc=tc, dk=dk, dv=dv),
        out_shape=jax.ShapeDtypeStruct((b, l, dv_total), BF16),
        grid=(b, l // tc),
        in_specs=[_row_spec(tc, n), _row_spec(tc, dk_total), _const_spec((1, dv))],
        out_specs=_row_spec(tc, dv_total),
        scratch_shapes=[pltpu.VMEM((GLA_HEADS, dv, dk), F32)],
        compiler_params=_params(("arbitrary", "arbitrary")),
        name="gla_scan",
    )(p, la, g_onorm.reshape(1, dv))
    return _proj_res(_proj_res_kernel, o, w_out.astype(BF16), x, mod, 2, tm, "gla_out")


def _mlstm_kernel(xz_ref, cw_ref, cb_ref, wq_ref, wk_ref, wv_ref, wif_ref, bif_ref, skip_ref, gn_ref,
                  o_ref, ext_ref, q_ref, k_ref, v_ref, xc_ref, gate_ref, c_ref, n_ref, m_ref, *, tc, inner):
    heads = MLSTM_HEADS
    dh = inner // heads
    conv_w = cw_ref.shape[0]
    halo = SUBLANES

    @pl.when(pl.program_id(1) == 0)
    def _():
        ext_ref[0:halo, :] = jnp.zeros((halo, inner), F32)
        c_ref[...] = jnp.zeros_like(c_ref)
        n_ref[...] = jnp.zeros_like(n_ref)
        m_ref[...] = jnp.zeros_like(m_ref)

    ext_ref[halo:halo + tc, :] = xz_ref[:, :inner].astype(F32)
    acc = jnp.broadcast_to(cb_ref[...], (tc, inner))
    for j in range(conv_w):
        off = halo - (conv_w - 1) + j
        acc = acc + cw_ref[j:j + 1, :] * ext_ref[off:off + tc, :]
    ext_ref[0:halo, :] = ext_ref[tc:tc + halo, :]
    xc = _silu(acc)
    xc_ref[...] = xc

    gates = jnp.broadcast_to(bif_ref[...], (tc, LANES))
    for i in range(inner // MXU):
        sl = slice(i * MXU, (i + 1) * MXU)
        xcb = xc[:, sl].astype(BF16)
        q = jnp.dot(xcb, wq_ref[i], preferred_element_type=F32)
        k = jnp.dot(xcb, wk_ref[i], preferred_element_type=F32)
        v = jnp.dot(xz_ref[:, sl], wv_ref[i], preferred_element_type=F32)
        qb, kb, vb = q.astype(BF16), k.astype(BF16), v.astype(BF16)
        q_ref[:, sl] = qb
        k_ref[:, sl] = (k * dh ** -0.5).astype(BF16)
        v_ref[:, sl] = vb
        gates = gates + jnp.dot(qb, wif_ref[i * MXU:(i + 1) * MXU, :], preferred_element_type=F32)
        gates = gates + jnp.dot(kb, wif_ref[inner + i * MXU:inner + (i + 1) * MXU, :], preferred_element_type=F32)
        gates = gates + jnp.dot(vb, wif_ref[2 * inner + i * MXU:2 * inner + (i + 1) * MXU, :],
                                preferred_element_type=F32)
    gate_ref[...] = gates

    rr = lax.broadcasted_iota(jnp.int32, (CHUNK, CHUNK), 0)
    cc = lax.broadcasted_iota(jnp.int32, (CHUNK, CHUNK), 1)
    tri = (rr >= cc).astype(F32)

    def chunk(j, carry):
        rows = pl.ds(pl.multiple_of(j * CHUNK, CHUNK), CHUNK)
        g = gate_ref[rows, :]
        cum_all = jnp.dot(tri, _log_sigmoid(g), preferred_element_type=F32, precision=HIGHEST)
        for h in range(heads):
            hs = slice(h * dh, (h + 1) * dh)
            gi = g[:, h:h + 1]
            cum = cum_all[:, heads + h:heads + h + 1]
            tot = cum[CHUNK - 1:CHUNK, :]
            logw = tot - cum + gi
            m_old = m_ref[h][:, 0:1]
            m_new = jnp.maximum(tot + m_old, jnp.max(logw, axis=0, keepdims=True))
            decay = jnp.exp(tot + m_old - m_new)
            kw = k_ref[rows, hs].astype(F32) * jnp.exp(logw - m_new)
            cmat = decay * c_ref[h] + lax.dot_general(kw.astype(BF16), v_ref[rows, hs], (((0,), (0,)), ((), ())),
                                                      preferred_element_type=F32)
            c_ref[h] = cmat
            nvec = decay * n_ref[h] + jnp.sum(kw, axis=0, keepdims=True)
            n_ref[h] = nvec
            m_ref[h] = jnp.broadcast_to(m_new, (1, LANES))
            q = q_ref[rows, hs]
            num = jnp.dot(q, cmat.astype(BF16), preferred_element_type=F32)
            qn = jnp.sum(q.astype(F32) * nvec, axis=-1, keepdims=True)
            den = jnp.maximum(jnp.abs(qn), jnp.exp(-m_new))
            hc = num / den
            mu = jnp.mean(hc, axis=-1, keepdims=True)
            var = jnp.mean((hc - mu) * (hc - mu), axis=-1, keepdims=True)
            hn = (hc - mu) * lax.rsqrt(var + EPS) * gn_ref[:, hs]
            z = xz_ref[rows, inner + h * dh:inner + (h + 1) * dh].astype(F32)
            o_ref[rows, hs] = ((hn + skip_ref[:, hs] * xc_ref[rows, hs]) * _silu(z)).astype(BF16)
        return carry

    lax.fori_loop(0, tc // CHUNK, chunk, 0)


def _diag_tiles(w):
    nblk, bs, _ = w.shape
    per = MXU // bs
    wt = w.reshape(nblk // per, per, bs, bs)
    eye = jnp.eye(per, dtype=w.dtype)
    return jnp.einsum("tpio,pq->tpiqo", wt, eye).reshape(nblk // per, MXU, MXU).astype(BF16)


def _mlstm_mixer(x, g_mix, mod, w_up, conv_w, conv_b, w_q, w_k, w_v, w_if, b_if, skip, g_norm, w_down, tiles):
    b, l, d = x.shape
    tm, tc = tiles["row"], tiles["mlstm"]
    inner = w_down.shape[0]
    heads = MLSTM_HEADS
    xz = _norm_proj(x, g_mix, mod, 1, 0, w_up.astype(BF16), BF16, tm)
    wif = jnp.zeros((3 * inner, LANES), BF16).at[:, :2 * heads].set(w_if.astype(BF16))
    bif = jnp.zeros((1, LANES), F32).at[0, :2 * heads].set(b_if)
    nt = inner // MXU
    y = pl.pallas_call(
        functools.partial(_mlstm_kernel, tc=tc, inner=inner),
        out_shape=jax.ShapeDtypeStruct((b, l, inner), BF16),
        grid=(b, l // tc),
        in_specs=[_row_spec(tc, 2 * inner), _const_spec(conv_w.shape), _const_spec((1, inner)),
                  _const_spec((nt, MXU, MXU)), _const_spec((nt, MXU, MXU)), _const_spec((nt, MXU, MXU)),
                  _const_spec((3 * inner, LANES)), _const_spec((1, LANES)), _const_spec((1, inner)),
                  _const_spec((1, inner))],
        out_specs=_row_spec(tc, inner),
        scratch_shapes=[pltpu.VMEM((tc + 2 * SUBLANES, inner), F32),
                        pltpu.VMEM((tc, inner), BF16), pltpu.VMEM((tc, inner), BF16),
                        pltpu.VMEM((tc, inner), BF16), pltpu.VMEM((tc, inner), F32),
                        pltpu.VMEM((tc, LANES), F32),
                        pltpu.VMEM((heads, inner // heads, inner // heads), F32),
                        pltpu.VMEM((heads, 1, inner // heads), F32),
                        pltpu.VMEM((heads, 1, LANES), F32)],
        compiler_params=_params(("arbitrary", "arbitrary")),
        name="mlstm_scan",
    )(xz, conv_w, conv_b.reshape(1, inner), _diag_tiles(w_q), _diag_tiles(w_k), _diag_tiles(w_v), wif, bif,
      skip.reshape(1, inner), g_norm.reshape(1, inner))
    return _proj_res(_proj_res_kernel, y, w_down.astype(BF16), x, mod, 2, tm, "mlstm_out")


def _s5_kernel(u_ref, bt_ref, ct_ref, are_ref, aim_ref, d_ref, y_ref, bu_ref, st_ref, *, ts, nb, d, npart, per):
    rows = ts * nb
    half = per // 2
    ntile = npart // MXU

    @pl.when(pl.program_id(0) == 0)
    def _():
        st_ref[...] = jnp.zeros_like(st_ref)

    def col_tile(t, j):
        return half * t + j if j < half else ntile + half * t + (j - half)

    u = u_ref[...].reshape(rows, d)
    ub = u.astype(BF16)
    for kt in range(d // MXU):
        lhs = ub[:, kt * MXU:(kt + 1) * MXU]
        for j in range(per):
            n = col_tile(kt, j)
            bu_ref[:, n * MXU:(n + 1) * MXU] = jnp.dot(lhs, bt_ref[kt * per + j], preferred_element_type=F32)

    cb = min(512, npart)
    for c in range(npart // cb):
        re = slice(c * cb, (c + 1) * cb)
        im = slice(npart + c * cb, npart + (c + 1) * cb)
        ar = are_ref[:, re]
        ai = aim_ref[:, re]

        def step(t, carry):
            xr, xi = carry
            r = pl.ds(pl.multiple_of(t * nb, nb), nb)
            nr = ar * xr - ai * xi + bu_ref[r, re]
            ni = ar * xi + ai * xr + bu_ref[r, im]
            bu_ref[r, re] = nr
            bu_ref[r, im] = ni
            return nr, ni

        xr, xi = lax.fori_loop(0, ts, step, (st_ref[:, re], st_ref[:, im]))
        st_ref[:, re] = xr
        st_ref[:, im] = xi

    for nt in range(d // MXU):
        acc = jnp.zeros((rows, MXU), F32)
        for j in range(per):
            n = col_tile(nt, j)
            acc = acc + jnp.dot(bu_ref[:, n * MXU:(n + 1) * MXU].astype(BF16), ct_ref[nt * per + j],
                                preferred_element_type=F32)
        ch = slice(nt * MXU, (nt + 1) * MXU)
        yv = acc + d_ref[:, ch] * u[:, ch]
        y_ref[:, :, ch] = jax.nn.gelu(yv).reshape(ts, nb, MXU)


def _s5_mixer(x, g_mix, mod, w_in, a_re, a_im, log_dt, b_re, b_im, c_re, c_im, d_skip, w_out, tiles):
    b, l, d = x.shape
    tm, ts = tiles["row"], tiles["s5"]
    groups, pst = a_re.shape
    npart = groups * pst
    gk = MXU // S5_GROUP
    gn = MXU // pst
    per = 2 * (gk // gn)
    nkt = d // MXU

    a = lax.complex(a_re, a_im)
    dt = jnp.exp(log_dt)[:, None]
    a_bar = jnp.exp(a * dt)
    b_bar = ((a_bar - 1.0) / a)[..., None] * lax.complex(b_re, b_im)
    eye = jnp.eye(groups, dtype=F32)
    bfull = jnp.concatenate([jnp.einsum("gpc,gh->gchp", part, eye).reshape(d, npart)
                             for part in (jnp.real(b_bar), jnp.imag(b_bar))], axis=1)
    cfull = jnp.concatenate([jnp.einsum("gcp,gh->gphc", part, eye).reshape(npart, d)
                             for part in (c_re, -c_im)], axis=0)
    half = per // 2
    ntile = npart // MXU

    def col_tile(t, j):
        return half * t + j if j < half else ntile + half * t + (j - half)

    bt = jnp.stack([bfull[kt * MXU:(kt + 1) * MXU, col_tile(kt, j) * MXU:(col_tile(kt, j) + 1) * MXU]
                    for kt in range(nkt) for j in range(per)]).astype(BF16)
    ct = jnp.stack([cfull[col_tile(nt, j) * MXU:(col_tile(nt, j) + 1) * MXU, nt * MXU:(nt + 1) * MXU]
                    for nt in range(nkt) for j in range(per)]).astype(BF16)
    are = jnp.broadcast_to(jnp.real(a_bar).reshape(1, npart), (b, npart))
    aim = jnp.broadcast_to(jnp.imag(a_bar).reshape(1, npart), (b, npart))

    u = _norm_proj(x, g_mix, mod, 1, 0, w_in.astype(BF16), F32, tm)
    ut = jnp.transpose(u, (1, 0, 2))
    yt = pl.pallas_call(
        functools.partial(_s5_kernel, ts=ts, nb=b, d=d, npart=npart, per=per),
        out_shape=jax.ShapeDtypeStruct((l, b, d), F32),
        grid=(l // ts,),
        in_specs=[pl.BlockSpec((ts, b, d), lambda t: (t, 0, 0)),
                  pl.BlockSpec((nkt * per, MXU, MXU), lambda t: (0, 0, 0)),
                  pl.BlockSpec((nkt * per, MXU, MXU), lambda t: (0, 0, 0)),
                  pl.BlockSpec((b, npart), lambda t: (0, 0)),
                  pl.BlockSpec((b, npart), lambda t: (0, 0)),
                  pl.BlockSpec((1, d), lambda t: (0, 0))],
        out_specs=pl.BlockSpec((ts, b, d), lambda t: (t, 0, 0)),
        scratch_shapes=[pltpu.VMEM((ts * b, 2 * npart), F32), pltpu.VMEM((b, 2 * npart), F32)],
        compiler_params=_params(("arbitrary",)),
        name="s5_scan",
    )(ut, bt, ct, are, aim, d_skip.reshape(1, d))
    y = jnp.transpose(yt, (1, 0, 2)).astype(BF16)
    return _proj_res(_glu_res_kernel, y, w_out.astype(BF16), x, mod, 2, tm, "s5_out")


def _router_kernel(x_ref, g_ref, sc_ref, sh_ref, wr_ref, br_ref, h_ref, ri_ref, rw_ref, cnt_ref, run_ref, *, tm):
    @pl.when((pl.program_id(0) == 0) & (pl.program_id(1) == 0))
    def _():
        run_ref[...] = jnp.zeros_like(run_ref)

    hb = _normmod(x_ref[...], g_ref[...], sc_ref[...], sh_ref[...]).astype(BF16)
    h_ref[...] = hb
    logits = jnp.dot(hb, wr_ref[...], preferred_element_type=F32) + br_ref[...]
    lane = lax.broadcasted_iota(jnp.int32, logits.shape, 1)
    lane_f = lane.astype(F32)
    work = logits
    vals, idxs = [], []
    for _ in range(TOP_K):
        m = jnp.max(work, axis=-1, keepdims=True)
        idx = jnp.min(jnp.where(work == m, lane_f, float(LANES)), axis=-1, keepdims=True)
        vals.append(m)
        idxs.append(idx)
        work = jnp.where(lane_f == idx, 2.0 * NEG_BIG, work)
    exps = [jnp.exp(v - vals[0]) for v in vals]
    denom = exps[0]
    for e in exps[1:]:
        denom = denom + e
    multi = jnp.zeros(logits.shape, F32)
    for idx in idxs:
        multi = multi + (lane_f == idx).astype(F32)
    rr = lax.broadcasted_iota(jnp.int32, (tm, tm), 0)
    cc = lax.broadcasted_iota(jnp.int32, (tm, tm), 1)
    before = (rr > cc).astype(BF16)
    cs = jnp.dot(before, multi.astype(BF16), preferred_element_type=F32) + run_ref[0:1, :]
    run = run_ref[0:1, :] + jnp.sum(multi, axis=0, keepdims=True)
    run_ref[...] = jnp.broadcast_to(run, run_ref.shape)
    cnt_ref[...] = jnp.broadcast_to(run, cnt_ref.shape).astype(jnp.int32)
    ri = jnp.zeros(logits.shape, F32)
    rw = jnp.zeros(logits.shape, F32)
    for k in range(TOP_K):
        rank = jnp.sum(jnp.where(lane_f == idxs[k], cs, 0.0), axis=-1, keepdims=True)
        ri = ri + jnp.where(lane == k, idxs[k], 0.0) + jnp.where(lane == TOP_K + k, rank, 0.0)
        rw = rw + jnp.where(lane == k, exps[k] / denom, 0.0)
    ri_ref[...] = ri.astype(jnp.int32)
    rw_ref[...] = rw


def _expert_kernel(te_ref, nv_ref, xs_ref, wgu_ref, bgu_ref, wd_ref, bd_ref, o_ref, act_ref, *, dff):
    i = pl.program_id(0)

    @pl.when(i < nv_ref[0])
    def _():
        x = xs_ref[...]
        for n in range(dff // MXU):
            gs = slice(n * MXU, (n + 1) * MXU)
            us = slice(dff + n * MXU, dff + (n + 1) * MXU)
            gate = jnp.dot(x, wgu_ref[0, :, gs], preferred_element_type=F32) + bgu_ref[0, :, gs]
            up = jnp.dot(x, wgu_ref[0, :, us], preferred_element_type=F32) + bgu_ref[0, :, us]
            gate = jnp.minimum(gate, SWIGLU_LIMIT)
            up = jnp.clip(up, -SWIGLU_LIMIT, SWIGLU_LIMIT)
            act_ref[:, gs] = ((up + 1.0) * gate * jax.nn.sigmoid(SWIGLU_ALPHA * gate)).astype(BF16)
        o_ref[...] = (jnp.dot(act_ref[...], wd_ref[0], preferred_element_type=F32) + bd_ref[0]).astype(o_ref.dtype)

    @pl.when(i >= nv_ref[0])
    def _():
        o_ref[...] = jnp.zeros_like(o_ref)


def _combine_kernel(yg_ref, rw_ref, x_ref, gt_ref, gf_ref, o_ref, *, final):
    d = o_ref.shape[-1]
    acc = jnp.zeros(o_ref.shape, F32)
    for k in range(TOP_K):
        acc = acc + rw_ref[:, k:k + 1] * yg_ref[:, k * d:(k + 1) * d].astype(F32)
    out = x_ref[...] + gt_ref[...] * acc
    if final:
        out = out * lax.rsqrt(jnp.mean(out * out, axis=-1, keepdims=True) + EPS) * gf_ref[...]
    o_ref[...] = out


def _moe(x, g_ffn, mod, w_router, b_router, w_gu, b_gu, w_down, b_down, g_final, final, tiles):
    b, l, d = x.shape
    t = b * l
    tm, te = tiles["row"], tiles["expert"]
    n_exp = w_router.shape[1]
    dff = w_down.shape[1]
    wr = jnp.zeros((d, LANES), BF16).at[:, :n_exp].set(w_router.astype(BF16))
    br = jnp.full((1, LANES), NEG_BIG, F32).at[0, :n_exp].set(b_router)
    h2, ri, rw, cnt = pl.pallas_call(
        functools.partial(_router_kernel, tm=tm),
        out_shape=(jax.ShapeDtypeStruct((b, l, d), BF16), jax.ShapeDtypeStruct((b, l, LANES), jnp.int32),
                   jax.ShapeDtypeStruct((b, l, LANES), F32), jax.ShapeDtypeStruct((SUBLANES, LANES), jnp.int32)),
        grid=(b, l // tm),
        in_specs=[_row_spec(tm, d), _const_spec((1, d)), _mod_spec(d, 4), _mod_spec(d, 3),
                  _const_spec((d, LANES)), _const_spec((1, LANES))],
        out_specs=(_row_spec(tm, d), _row_spec(tm, LANES), _row_spec(tm, LANES), _const_spec((SUBLANES, LANES))),
        scratch_shapes=[pltpu.VMEM((SUBLANES, LANES), F32)],
        compiler_params=_params(("arbitrary", "arbitrary")),
        name="moe_router",
    )(x, g_ffn.reshape(1, d), mod, mod, wr, br)

    ri = ri.reshape(t, LANES)
    idx, rank = ri[:, :TOP_K], ri[:, TOP_K:2 * TOP_K]
    counts = cnt[0, :n_exp]
    padded = ((counts + te - 1) // te) * te
    ends = jnp.cumsum(padded)
    dest = ((ends - padded)[idx] + rank).reshape(t * TOP_K)
    n_rows = t * TOP_K + n_exp * te
    n_tiles = n_rows // te
    src = jnp.zeros((n_rows,), jnp.int32).at[dest].set(jnp.arange(t * TOP_K, dtype=jnp.int32) // TOP_K)
    tile_exp = jnp.minimum(jnp.searchsorted(ends, jnp.arange(n_tiles, dtype=jnp.int32) * te, side="right"),
                           n_exp - 1).astype(jnp.int32)
    n_valid = (ends[-1:] // te).astype(jnp.int32)
    xs = jnp.take(h2.reshape(t, d), src, axis=0)

    ys = pl.pallas_call(
        functools.partial(_expert_kernel, dff=dff),
        out_shape=jax.ShapeDtypeStruct((n_rows, d), BF16),
        grid_spec=pltpu.PrefetchScalarGridSpec(
            num_scalar_prefetch=2,
            grid=(n_tiles,),
            in_specs=[pl.BlockSpec((te, d), lambda i, e, nv: (i, 0)),
                      pl.BlockSpec((1, d, 2 * dff), lambda i, e, nv: (e[i], 0, 0)),
                      pl.BlockSpec((1, 1, 2 * dff), lambda i, e, nv: (e[i], 0, 0)),
                      pl.BlockSpec((1, dff, d), lambda i, e, nv: (e[i], 0, 0)),
                      pl.BlockSpec((1, 1, d), lambda i, e, nv: (e[i], 0, 0))],
            out_specs=pl.BlockSpec((te, d), lambda i, e, nv: (i, 0)),
            scratch_shapes=[pltpu.VMEM((te, dff), BF16)]),
        compiler_params=_params(("arbitrary",)),
        name="moe_experts",
    )(tile_exp, n_valid, xs, w_gu.astype(BF16), b_gu.reshape(n_exp, 1, 2 * dff), w_down.astype(BF16),
      b_down.reshape(n_exp, 1, d))

    yg = jnp.take(ys, dest, axis=0).reshape(b, l, TOP_K * d)
    return pl.pallas_call(
        functools.partial(_combine_kernel, final=final),
        out_shape=jax.ShapeDtypeStruct((b, l, d), F32),
        grid=(b, l // tm),
        in_specs=[_row_spec(tm, TOP_K * d), _row_spec(tm, LANES), _row_spec(tm, d), _mod_spec(d, 5),
                  _const_spec((1, d))],
        out_specs=_row_spec(tm, d),
        compiler_params=_params(("parallel", "parallel")),
        name="moe_combine",
    )(yg, rw, x, mod, g_final.reshape(1, d))


def kernel(x, c, g_mix, g_ffn, w_ada, b_ada, gla_w_in, gla_w_gk1, gla_w_gk2, gla_b_gk, gla_g_onorm, gla_w_out, ml_w_up, ml_conv_w, ml_conv_b, ml_w_q, ml_w_k, ml_w_v, ml_w_if, ml_b_if, ml_skip, ml_g_norm, ml_w_down, s5_w_in, s5_a_re, s5_a_im, s5_log_dt, s5_b_re, s5_b_im, s5_c_re, s5_c_im, s5_d, s5_w_out, moe_w_router, moe_b_router, moe_w_gu, moe_b_gu, moe_w_down, moe_b_down, g_final):
    b, l, d = x.shape
    depth = w_ada.shape[0]
    n_exp = moe_w_router.shape[-1]
    assert l % CHUNK == 0 and d % MXU == 0
    tiles = _tiles(l, b * l, n_exp)
    mods = _modulation(c, w_ada, b_ada)
    for i in range(depth):
        mod = mods[i].reshape(b, 6, 1, d)
        kind, j = i % N_MIXERS, i // N_MIXERS
        if kind == 0:
            x = _gla_mixer(x, g_mix[i], mod, gla_w_in[j], gla_w_gk1[j], gla_w_gk2[j], gla_b_gk[j],
                           gla_g_onorm[j], gla_w_out[j], tiles)
        elif kind == 1:
            x = _mlstm_mixer(x, g_mix[i], mod, ml_w_up[j], ml_conv_w[j], ml_conv_b[j], ml_w_q[j], ml_w_k[j],
                             ml_w_v[j], ml_w_if[j], ml_b_if[j], ml_skip[j], ml_g_norm[j], ml_w_down[j], tiles)
        else:
            x = _s5_mixer(x, g_mix[i], mod, s5_w_in[j], s5_a_re[j], s5_a_im[j], s5_log_dt[j], s5_b_re[j],
                          s5_b_im[j], s5_c_re[j], s5_c_im[j], s5_d[j], s5_w_out[j], tiles)
        x = _moe(x, g_ffn[i], mod, moe_w_router[i], moe_b_router[i], moe_w_gu[i], moe_b_gu[i], moe_w_down[i],
                 moe_b_down[i], g_final, i == depth - 1, tiles)
    return x
```

```python
import functools

import jax
import jax.numpy as jnp
from jax import lax
from jax.experimental import pallas as pl
from jax.experimental.pallas import tpu as pltpu

F32 = jnp.float32
BF16 = jnp.bfloat16
HIGHEST = lax.Precision.HIGHEST

EPS = 1e-6
CHUNK = 64
N_MIXERS = 3
GLA_HEADS = 4
GLA_GATE_NORM = 16.0
MLSTM_HEADS = 4
S5_GROUP = 16
TOP_K = 4
SWIGLU_LIMIT = 7.0
SWIGLU_ALPHA = 1.702

LANES = 128
SUBLANES = 8
MXU = 256
VMEM_LIMIT_BYTES = 48 * 1024 * 1024

NEG_BIG = -1e30


def _tiles(seq, tokens, n_experts):
    row = min(512, seq)
    gla = min(256, seq)
    gla_batch = 4
    mlstm = min(256, seq)
    s5 = min(32, seq)
    per_expert = tokens * TOP_K // n_experts
    expert = 512 if per_expert >= 2048 else 128
    return dict(row=row, gla=gla, gla_batch=gla_batch, mlstm=mlstm, s5=s5, expert=expert)


def _params(sem):
    return pltpu.CompilerParams(dimension_semantics=sem, vmem_limit_bytes=VMEM_LIMIT_BYTES)


def _normmod(x, g, sc, sh):
    y = x * lax.rsqrt(jnp.mean(x * x, axis=-1, keepdims=True) + EPS)
    return (y * g) * (1.0 + sc) + sh


def _log_sigmoid(x):
    return jnp.minimum(x, 0.0) - jnp.log(1.0 + jnp.exp(-jnp.abs(x)))


def _silu(x):
    return x * jax.nn.sigmoid(x)


def _mod_spec(d, j):
    return pl.BlockSpec((None, None, 1, d), lambda b, t: (b, j, 0, 0))


def _row_spec(tm, n):
    return pl.BlockSpec((None, tm, n), lambda b, t: (b, t, 0))


def _const_spec(shape):
    nd = len(shape)
    return pl.BlockSpec(shape, lambda b, t: (0,) * nd)


def _mod_kernel(c_ref, w_ref, b_ref, o_ref):
    ca = _silu(c_ref[...]).astype(BF16)
    o_ref[0] = jnp.dot(ca, w_ref[0].astype(BF16), preferred_element_type=F32) + b_ref[0]


def _modulation(c, w_ada, b_ada):
    depth, d, n = w_ada.shape
    b = c.shape[0]
    tn = n // 4
    return pl.pallas_call(
        _mod_kernel,
        out_shape=jax.ShapeDtypeStruct((depth, b, n), F32),
        grid=(depth, n // tn),
        in_specs=[pl.BlockSpec((b, d), lambda i, j: (0, 0)),
                  pl.BlockSpec((1, d, tn), lambda i, j: (i, 0, j)),
                  pl.BlockSpec((1, 1, tn), lambda i, j: (i, 0, j))],
        out_specs=pl.BlockSpec((1, b, tn), lambda i, j: (i, 0, j)),
        compiler_params=_params(("parallel", "parallel")),
        name="adaln_mod",
    )(c, w_ada, b_ada.reshape(depth, 1, n))


def _norm_proj_kernel(x_ref, g_ref, sc_ref, sh_ref, w_ref, o_ref):
    h = _normmod(x_ref[...], g_ref[...], sc_ref[...], sh_ref[...]).astype(BF16)
    o_ref[...] = jnp.dot(h, w_ref[...], preferred_element_type=F32).astype(o_ref.dtype)


def _norm_proj(x, g, mod, jsc, jsh, w, out_dtype, tm):
    b, l, d = x.shape
    n = w.shape[1]
    return pl.pallas_call(
        _norm_proj_kernel,
        out_shape=jax.ShapeDtypeStruct((b, l, n), out_dtype),
        grid=(b, l // tm),
        in_specs=[_row_spec(tm, d), _const_spec((1, d)), _mod_spec(d, jsc), _mod_spec(d, jsh),
                  _const_spec((d, n))],
        out_specs=_row_spec(tm, n),
        compiler_params=_params(("parallel", "parallel")),
        name="norm_proj",
    )(x, g.reshape(1, d), mod, mod, w)


def _proj_res_kernel(y_ref, w_ref, x_ref, gt_ref, o_ref):
    o_ref[...] = x_ref[...] + gt_ref[...] * jnp.dot(y_ref[...], w_ref[...], preferred_element_type=F32)


def _glu_res_kernel(y_ref, w_ref, x_ref, gt_ref, o_ref):
    d = o_ref.shape[-1]
    glu = jnp.dot(y_ref[...], w_ref[...], preferred_element_type=F32)
    o_ref[...] = x_ref[...] + gt_ref[...] * (glu[:, :d] * jax.nn.sigmoid(glu[:, d:]))


def _proj_res(kernel, y, w, x, mod, jgt, tm, name):
    b, l, d = x.shape
    k = y.shape[-1]
    n = w.shape[1]
    return pl.pallas_call(
        kernel,
        out_shape=jax.ShapeDtypeStruct((b, l, d), F32),
        grid=(b, l // tm),
        in_specs=[_row_spec(tm, k), _const_spec((k, n)), _row_spec(tm, d), _mod_spec(d, jgt)],
        out_specs=_row_spec(tm, d),
        compiler_params=_params(("parallel", "parallel")),
        name=name,
    )(y, w, x, mod)


def _gla_proj_kernel(x_ref, g_ref, sc_ref, sh_ref, w_ref, w1_ref, w2_ref, b2_ref, p_ref, la_ref, *, dk_total, qscale):
    h = _normmod(x_ref[...], g_ref[...], sc_ref[...], sh_ref[...]).astype(BF16)
    proj = jnp.dot(h, w_ref[...], preferred_element_type=F32)
    p_ref[:, :dk_total] = (proj[:, :dk_total] * qscale).astype(BF16)
    p_ref[:, dk_total:] = proj[:, dk_total:].astype(BF16)
    t1 = jnp.dot(h, w1_ref[...], preferred_element_type=F32).astype(BF16)
    gk = jnp.dot(t1, w2_ref[...], preferred_element_type=F32) + b2_ref[...]
    la_ref[...] = _log_sigmoid(gk) * (1.0 / GLA_GATE_NORM)


def _gla_scan_kernel(p_ref, la_ref, gon_ref, o_ref, st_ref, *, nb, tc, dk, dv):
    heads = GLA_HEADS
    dk_total, dv_total = heads * dk, heads * dv

    @pl.when(pl.program_id(1) == 0)
    def _():
        st_ref[...] = jnp.zeros_like(st_ref)

    rr = lax.broadcasted_iota(jnp.int32, (CHUNK, CHUNK), 0)
    cc = lax.broadcasted_iota(jnp.int32, (CHUNK, CHUNK), 1)
    tri = (rr >= cc).astype(F32)

    def chunk(j, carry):
        rows = pl.ds(pl.multiple_of(j * CHUNK, CHUNK), CHUNK)
        for bi in range(nb):
            cum_all = jnp.dot(tri, la_ref[bi, rows, :], preferred_element_type=F32, precision=HIGHEST)
            for h in range(heads):
                ks = slice(h * dk, (h + 1) * dk)
                vs = slice(h * dv, (h + 1) * dv)
                cum = cum_all[:, ks]
                tot = cum[CHUNK - 1:CHUNK, :]
                kblk = p_ref[bi, rows, dk_total + h * dk:dk_total + (h + 1) * dk].astype(F32)
                kd = (kblk * jnp.exp(tot - cum)).astype(BF16)
                v = p_ref[bi, rows, 2 * dk_total + h * dv:2 * dk_total + (h + 1) * dv]
                upd = lax.dot_general(v, kd, (((0,), (0,)), ((), ())), preferred_element_type=F32)
                st = jnp.exp(tot) * st_ref[bi, h] + upd
                st_ref[bi, h] = st
                q = p_ref[bi, rows, ks]
                o = lax.dot_general(q, st.astype(BF16), (((1,), (1,)), ((), ())), preferred_element_type=F32)
                o = o * lax.rsqrt(jnp.mean(o * o, axis=-1, keepdims=True) + EPS) * gon_ref[...]
                g0 = 2 * dk_total + dv_total + h * dv
                g = p_ref[bi, rows, g0:g0 + dv].astype(F32)
                o_ref[bi, rows, vs] = (o * _silu(g)).astype(BF16)
        return carry

    lax.fori_loop(0, tc // CHUNK, chunk, 0)


def _gla_mixer(x, g_mix, mod, w_in, w_gk1, w_gk2, b_gk, g_onorm, w_out, tiles):
    b, l, d = x.shape
    tm, tc = tiles["row"], tiles["gla"]
    rank = w_gk1.shape[1]
    dk_total = w_gk2.shape[1]
    dv_total = w_out.shape[0]
    dk, dv = dk_total // GLA_HEADS, dv_total // GLA_HEADS
    n = w_in.shape[1]
    w1 = jnp.zeros((d, LANES), BF16).at[:, :rank].set(w_gk1.astype(BF16))
    w2 = jnp.zeros((LANES, dk_total), BF16).at[:rank, :].set(w_gk2.astype(BF16))
    p, la = pl.pallas_call(
        functools.partial(_gla_proj_kernel, dk_total=dk_total, qscale=dk ** -0.5),
        out_shape=(jax.ShapeDtypeStruct((b, l, n), BF16), jax.ShapeDtypeStruct((b, l, dk_total), F32)),
        grid=(b, l // tm),
        in_specs=[_row_spec(tm, d), _const_spec((1, d)), _mod_spec(d, 1), _mod_spec(d, 0),
                  _const_spec((d, n)), _const_spec((d, LANES)), _const_spec((LANES, dk_total)),
                  _const_spec((1, dk_total))],
        out_specs=(_row_spec(tm, n), _row_spec(tm, dk_total)),
        compiler_params=_params(("parallel", "parallel")),
        name="gla_proj",
    )(x, g_mix.reshape(1, d), mod, mod, w_in.astype(BF16), w1, w2, b_gk.reshape(1, dk_total))
    nb = tiles["gla_batch"] if b % tiles["gla_batch"] == 0 else 1

    def scan_spec(width):
        return pl.BlockSpec((nb, tc, width), lambda bi, ti: (bi, ti, 0))

    o = pl.pallas_call(
        functools.partial(_gla_scan_kernel, nb=nb, tc=tc, dk=dk, dv=dv),
        out_shape=jax.ShapeDtypeStruct((b, l, dv_total), BF16),
        grid=(b // nb, l // tc),
        in_specs=[scan_spec(n), scan_spec(dk_total), _const_spec((1, dv))],
        out_specs=scan_spec(dv_total),
        scratch_shapes=[pltpu.VMEM((nb, GLA_HEADS, dv, dk), F32)],
        compiler_params=_params(("arbitrary", "arbitrary")),
        name="gla_scan",
    )(p, la, g_onorm.reshape(1, dv))
    return _proj_res(_proj_res_kernel, o, w_out.astype(BF16), x, mod, 2, tm, "gla_out")


def _mlstm_kernel(xz_ref, cw_ref, cb_ref, wq_ref, wk_ref, wv_ref, wif_ref, bif_ref, skip_ref, gn_ref,
                  o_ref, ext_ref, q_ref, k_ref, v_ref, xc_ref, gate_ref, c_ref, n_ref, m_ref, *, tc, inner):
    heads = MLSTM_HEADS
    dh = inner // heads
    conv_w = cw_ref.shape[0]
    halo = SUBLANES

    @pl.when(pl.program_id(1) == 0)
    def _():
        ext_ref[0:halo, :] = jnp.zeros((halo, inner), F32)
        c_ref[...] = jnp.zeros_like(c_ref)
        n_ref[...] = jnp.zeros_like(n_ref)
        m_ref[...] = jnp.zeros_like(m_ref)

    ext_ref[halo:halo + tc, :] = xz_ref[:, :inner].astype(F32)
    acc = jnp.broadcast_to(cb_ref[...], (tc, inner))
    for j in range(conv_w):
        off = halo - (conv_w - 1) + j
        acc = acc + cw_ref[j:j + 1, :] * ext_ref[off:off + tc, :]
    ext_ref[0:halo, :] = ext_ref[tc:tc + halo, :]
    xc = _silu(acc)
    xc_ref[...] = xc

    gates = jnp.broadcast_to(bif_ref[...], (tc, LANES))
    for i in range(inner // MXU):
        sl = slice(i * MXU, (i + 1) * MXU)
        xcb = xc[:, sl].astype(BF16)
        q = jnp.dot(xcb, wq_ref[i], preferred_element_type=F32)
        k = jnp.dot(xcb, wk_ref[i], preferred_element_type=F32)
        v = jnp.dot(xz_ref[:, sl], wv_ref[i], preferred_element_type=F32)
        qb, kb, vb = q.astype(BF16), k.astype(BF16), v.astype(BF16)
        q_ref[:, sl] = qb
        k_ref[:, sl] = (k * dh ** -0.5).astype(BF16)
        v_ref[:, sl] = vb
        gates = gates + jnp.dot(qb, wif_ref[i * MXU:(i + 1) * MXU, :], preferred_element_type=F32)
        gates = gates + jnp.dot(kb, wif_ref[inner + i * MXU:inner + (i + 1) * MXU, :], preferred_element_type=F32)
        gates = gates + jnp.dot(vb, wif_ref[2 * inner + i * MXU:2 * inner + (i + 1) * MXU, :],
                                preferred_element_type=F32)
    gate_ref[...] = gates

    rr = lax.broadcasted_iota(jnp.int32, (CHUNK, CHUNK), 0)
    cc = lax.broadcasted_iota(jnp.int32, (CHUNK, CHUNK), 1)
    tri = (rr >= cc).astype(F32)

    def chunk(j, carry):
        rows = pl.ds(pl.multiple_of(j * CHUNK, CHUNK), CHUNK)
        g = gate_ref[rows, :]
        cum_all = jnp.dot(tri, _log_sigmoid(g), preferred_element_type=F32, precision=HIGHEST)
        for h in range(heads):
            hs = slice(h * dh, (h + 1) * dh)
            gi = g[:, h:h + 1]
            cum = cum_all[:, heads + h:heads + h + 1]
            tot = cum[CHUNK - 1:CHUNK, :]
            logw = tot - cum + gi
            m_old = m_ref[h][:, 0:1]
            m_new = jnp.maximum(tot + m_old, jnp.max(logw, axis=0, keepdims=True))
            decay = jnp.exp(tot + m_old - m_new)
            kw = k_ref[rows, hs].astype(F32) * jnp.exp(logw - m_new)
            cmat = decay * c_ref[h] + lax.dot_general(kw.astype(BF16), v_ref[rows, hs], (((0,), (0,)), ((), ())),
                                                      preferred_element_type=F32)
            c_ref[h] = cmat
            nvec = decay * n_ref[h] + jnp.sum(kw, axis=0, keepdims=True)
            n_ref[h] = nvec
            m_ref[h] = jnp.broadcast_to(m_new, (1, LANES))
            q = q_ref[rows, hs]
            num = jnp.dot(q, cmat.astype(BF16), preferred_element_type=F32)
            qn = jnp.sum(q.astype(F32) * nvec, axis=-1, keepdims=True)
            den = jnp.maximum(jnp.abs(qn), jnp.exp(-m_new))
            hc = num / den
            mu = jnp.mean(hc, axis=-1, keepdims=True)
            var = jnp.mean((hc - mu) * (hc - mu), axis=-1, keepdims=True)
            hn = (hc - mu) * lax.rsqrt(var + EPS) * gn_ref[:, hs]
            z = xz_ref[rows, inner + h * dh:inner + (h + 1) * dh].astype(F32)
            o_ref[rows, hs] = ((hn + skip_ref[:, hs] * xc_ref[rows, hs]) * _silu(z)).astype(BF16)
        return carry

    lax.fori_loop(0, tc // CHUNK, chunk, 0)


def _diag_tiles(w):
    nblk, bs, _ = w.shape
    per = MXU // bs
    wt = w.reshape(nblk // per, per, bs, bs)
    eye = jnp.eye(per, dtype=w.dtype)
    return jnp.einsum("tpio,pq->tpiqo", wt, eye).reshape(nblk // per, MXU, MXU).astype(BF16)


def _mlstm_mixer(x, g_mix, mod, w_up, conv_w, conv_b, w_q, w_k, w_v, w_if, b_if, skip, g_norm, w_down, tiles):
    b, l, d = x.shape
    tm, tc = tiles["row"], tiles["mlstm"]
    inner = w_down.shape[0]
    heads = MLSTM_HEADS
    xz = _norm_proj(x, g_mix, mod, 1, 0, w_up.astype(BF16), BF16, tm)
    wif = jnp.zeros((3 * inner, LANES), BF16).at[:, :2 * heads].set(w_if.astype(BF16))
    bif = jnp.zeros((1, LANES), F32).at[0, :2 * heads].set(b_if)
    nt = inner // MXU
    y = pl.pallas_call(
        functools.partial(_mlstm_kernel, tc=tc, inner=inner),
        out_shape=jax.ShapeDtypeStruct((b, l, inner), BF16),
        grid=(b, l // tc),
        in_specs=[_row_spec(tc, 2 * inner), _const_spec(conv_w.shape), _const_spec((1, inner)),
                  _const_spec((nt, MXU, MXU)), _const_spec((nt, MXU, MXU)), _const_spec((nt, MXU, MXU)),
                  _const_spec((3 * inner, LANES)), _const_spec((1, LANES)), _const_spec((1, inner)),
                  _const_spec((1, inner))],
        out_specs=_row_spec(tc, inner),
        scratch_shapes=[pltpu.VMEM((tc + 2 * SUBLANES, inner), F32),
                        pltpu.VMEM((tc, inner), BF16), pltpu.VMEM((tc, inner), BF16),
                        pltpu.VMEM((tc, inner), BF16), pltpu.VMEM((tc, inner), F32),
                        pltpu.VMEM((tc, LANES), F32),
                        pltpu.VMEM((heads, inner // heads, inner // heads), F32),
                        pltpu.VMEM((heads, 1, inner // heads), F32),
                        pltpu.VMEM((heads, 1, LANES), F32)],
        compiler_params=_params(("arbitrary", "arbitrary")),
        name="mlstm_scan",
    )(xz, conv_w, conv_b.reshape(1, inner), _diag_tiles(w_q), _diag_tiles(w_k), _diag_tiles(w_v), wif, bif,
      skip.reshape(1, inner), g_norm.reshape(1, inner))
    return _proj_res(_proj_res_kernel, y, w_down.astype(BF16), x, mod, 2, tm, "mlstm_out")


def _s5_kernel(u_ref, bt_ref, ct_ref, are_ref, aim_ref, d_ref, y_ref, bu_ref, st_ref, *, ts, nb, d, npart, per):
    rows = ts * nb
    half = per // 2
    ntile = npart // MXU

    @pl.when(pl.program_id(0) == 0)
    def _():
        st_ref[...] = jnp.zeros_like(st_ref)

    def col_tile(t, j):
        return half * t + j if j < half else ntile + half * t + (j - half)

    u = u_ref[...].reshape(rows, d)
    ub = u.astype(BF16)
    for kt in range(d // MXU):
        lhs = ub[:, kt * MXU:(kt + 1) * MXU]
        for j in range(per):
            n = col_tile(kt, j)
            bu_ref[:, n * MXU:(n + 1) * MXU] = jnp.dot(lhs, bt_ref[kt * per + j], preferred_element_type=F32)

    cb = min(512, npart)
    for c in range(npart // cb):
        re = slice(c * cb, (c + 1) * cb)
        im = slice(npart + c * cb, npart + (c + 1) * cb)
        ar = are_ref[:, re]
        ai = aim_ref[:, re]

        def step(t, carry):
            xr, xi = carry
            r = pl.ds(pl.multiple_of(t * nb, nb), nb)
            nr = ar * xr - ai * xi + bu_ref[r, re]
            ni = ar * xi + ai * xr + bu_ref[r, im]
            bu_ref[r, re] = nr
            bu_ref[r, im] = ni
            return nr, ni

        xr, xi = lax.fori_loop(0, ts, step, (st_ref[:, re], st_ref[:, im]))
        st_ref[:, re] = xr
        st_ref[:, im] = xi

    for nt in range(d // MXU):
        acc = jnp.zeros((rows, MXU), F32)
        for j in range(per):
            n = col_tile(nt, j)
            acc = acc + jnp.dot(bu_ref[:, n * MXU:(n + 1) * MXU].astype(BF16), ct_ref[nt * per + j],
                                preferred_element_type=F32)
        ch = slice(nt * MXU, (nt + 1) * MXU)
        yv = acc + d_ref[:, ch] * u[:, ch]
        y_ref[:, :, ch] = jax.nn.gelu(yv).reshape(ts, nb, MXU)


def _s5_mixer(xs, g_mix, mods, w_in, a_re, a_im, log_dt, b_re, b_im, c_re, c_im, d_skip, w_out, tiles):
    l, d = xs[0].shape[1:]
    b = sum(p.shape[0] for p in xs)
    tm, ts = tiles["row"], tiles["s5"]
    groups, pst = a_re.shape
    npart = groups * pst
    gk = MXU // S5_GROUP
    gn = MXU // pst
    per = 2 * (gk // gn)
    nkt = d // MXU

    a = lax.complex(a_re, a_im)
    dt = jnp.exp(log_dt)[:, None]
    a_bar = jnp.exp(a * dt)
    b_bar = ((a_bar - 1.0) / a)[..., None] * lax.complex(b_re, b_im)
    eye = jnp.eye(groups, dtype=F32)
    bfull = jnp.concatenate([jnp.einsum("gpc,gh->gchp", part, eye).reshape(d, npart)
                             for part in (jnp.real(b_bar), jnp.imag(b_bar))], axis=1)
    cfull = jnp.concatenate([jnp.einsum("gcp,gh->gphc", part, eye).reshape(npart, d)
                             for part in (c_re, -c_im)], axis=0)
    half = per // 2
    ntile = npart // MXU

    def col_tile(t, j):
        return half * t + j if j < half else ntile + half * t + (j - half)

    bt = jnp.stack([bfull[kt * MXU:(kt + 1) * MXU, col_tile(kt, j) * MXU:(col_tile(kt, j) + 1) * MXU]
                    for kt in range(nkt) for j in range(per)]).astype(BF16)
    ct = jnp.stack([cfull[col_tile(nt, j) * MXU:(col_tile(nt, j) + 1) * MXU, nt * MXU:(nt + 1) * MXU]
                    for nt in range(nkt) for j in range(per)]).astype(BF16)
    are = jnp.broadcast_to(jnp.real(a_bar).reshape(1, npart), (b, npart))
    aim = jnp.broadcast_to(jnp.imag(a_bar).reshape(1, npart), (b, npart))

    win = w_in.astype(BF16)
    u = jnp.concatenate([_norm_proj(x, g_mix, mod, 1, 0, win, F32, tm) for x, mod in zip(xs, mods)], axis=0)
    ut = jnp.transpose(u, (1, 0, 2))
    yt = pl.pallas_call(
        functools.partial(_s5_kernel, ts=ts, nb=b, d=d, npart=npart, per=per),
        out_shape=jax.ShapeDtypeStruct((l, b, d), F32),
        grid=(l // ts,),
        in_specs=[pl.BlockSpec((ts, b, d), lambda t: (t, 0, 0)),
                  pl.BlockSpec((nkt * per, MXU, MXU), lambda t: (0, 0, 0)),
                  pl.BlockSpec((nkt * per, MXU, MXU), lambda t: (0, 0, 0)),
                  pl.BlockSpec((b, npart), lambda t: (0, 0)),
                  pl.BlockSpec((b, npart), lambda t: (0, 0)),
                  pl.BlockSpec((1, d), lambda t: (0, 0))],
        out_specs=pl.BlockSpec((ts, b, d), lambda t: (t, 0, 0)),
        scratch_shapes=[pltpu.VMEM((ts * b, 2 * npart), F32), pltpu.VMEM((b, 2 * npart), F32)],
        compiler_params=_params(("arbitrary",)),
        name="s5_scan",
    )(ut, bt, ct, are, aim, d_skip.reshape(1, d))
    y = jnp.transpose(yt, (1, 0, 2)).astype(BF16)
    wout = w_out.astype(BF16)
    outs, b0 = [], 0
    for x, mod in zip(xs, mods):
        outs.append(_proj_res(_glu_res_kernel, y[b0:b0 + x.shape[0]], wout, x, mod, 2, tm, "s5_out"))
        b0 += x.shape[0]
    return outs


def _router_kernel(x_ref, g_ref, sc_ref, sh_ref, wr_ref, br_ref, h_ref, ri_ref, rw_ref, cnt_ref, run_ref, *, tm):
    @pl.when((pl.program_id(0) == 0) & (pl.program_id(1) == 0))
    def _():
        run_ref[...] = jnp.zeros_like(run_ref)

    hb = _normmod(x_ref[...], g_ref[...], sc_ref[...], sh_ref[...]).astype(BF16)
    h_ref[...] = hb
    logits = jnp.dot(hb, wr_ref[...], preferred_element_type=F32) + br_ref[...]
    lane = lax.broadcasted_iota(jnp.int32, logits.shape, 1)
    lane_f = lane.astype(F32)
    work = logits
    vals, idxs = [], []
    for _ in range(TOP_K):
        m = jnp.max(work, axis=-1, keepdims=True)
        idx = jnp.min(jnp.where(work == m, lane_f, float(LANES)), axis=-1, keepdims=True)
        vals.append(m)
        idxs.append(idx)
        work = jnp.where(lane_f == idx, 2.0 * NEG_BIG, work)
    exps = [jnp.exp(v - vals[0]) for v in vals]
    denom = exps[0]
    for e in exps[1:]:
        denom = denom + e
    multi = jnp.zeros(logits.shape, F32)
    for idx in idxs:
        multi = multi + (lane_f == idx).astype(F32)
    rr = lax.broadcasted_iota(jnp.int32, (tm, tm), 0)
    cc = lax.broadcasted_iota(jnp.int32, (tm, tm), 1)
    before = (rr > cc).astype(BF16)
    cs = jnp.dot(before, multi.astype(BF16), preferred_element_type=F32) + run_ref[0:1, :]
    run = run_ref[0:1, :] + jnp.sum(multi, axis=0, keepdims=True)
    run_ref[...] = jnp.broadcast_to(run, run_ref.shape)
    cnt_ref[...] = jnp.broadcast_to(run, cnt_ref.shape).astype(jnp.int32)
    ri = jnp.zeros(logits.shape, F32)
    rw = jnp.zeros(logits.shape, F32)
    for k in range(TOP_K):
        rank = jnp.sum(jnp.where(lane_f == idxs[k], cs, 0.0), axis=-1, keepdims=True)
        ri = ri + jnp.where(lane == k, idxs[k], 0.0) + jnp.where(lane == TOP_K + k, rank, 0.0)
        rw = rw + jnp.where(lane == k, exps[k] / denom, 0.0)
    ri_ref[...] = ri.T[:2 * TOP_K, :].astype(jnp.int32)
    rw_ref[...] = rw


def _dest_kernel(offs_ref, ri_ref, o_ref, *, n_exp):
    idx = ri_ref[0:TOP_K, :]
    off = jnp.zeros(idx.shape, jnp.int32)
    for e in range(n_exp):
        off = jnp.where(idx == e, offs_ref[e], off)
    o_ref[0:TOP_K, :] = off + ri_ref[TOP_K:2 * TOP_K, :]
    o_ref[TOP_K:2 * TOP_K, :] = idx


def _cast_kernel(w_ref, o_ref):
    o_ref[...] = w_ref[...].astype(o_ref.dtype)


def _cast_layer_bf16(w, layer):
    _, n_exp, r, c = w.shape
    return pl.pallas_call(
        _cast_kernel,
        out_shape=jax.ShapeDtypeStruct((n_exp, r, c), BF16),
        grid=(n_exp,),
        in_specs=[pl.BlockSpec((None, None, r, c), lambda e: (layer, e, 0, 0))],
        out_specs=pl.BlockSpec((None, r, c), lambda e: (e, 0, 0)),
        compiler_params=_params(("parallel",)),
        name="moe_weight_cast",
    )(w)


def _expert_kernel(te_ref, nv_ref, xs_ref, wgu_ref, bgu_ref, wd_ref, bd_ref, o_ref, act_ref, *, dff):
    valid = pl.program_id(0) < nv_ref[0]

    @pl.when(valid)
    def _():
        x = xs_ref[...]
        for n in range(dff // MXU):
            gs = slice(n * MXU, (n + 1) * MXU)
            us = slice(dff + n * MXU, dff + (n + 1) * MXU)
            gate = jnp.dot(x, wgu_ref[:, gs], preferred_element_type=F32) + bgu_ref[:, gs]
            up = jnp.dot(x, wgu_ref[:, us], preferred_element_type=F32) + bgu_ref[:, us]
            gate = jnp.minimum(gate, SWIGLU_LIMIT)
            up = jnp.clip(up, -SWIGLU_LIMIT, SWIGLU_LIMIT)
            act_ref[:, gs] = ((up + 1.0) * gate * jax.nn.sigmoid(SWIGLU_ALPHA * gate)).astype(BF16)
        o_ref[...] = (jnp.dot(act_ref[...], wd_ref[...], preferred_element_type=F32) + bd_ref[...]).astype(o_ref.dtype)

    @pl.when(jnp.logical_not(valid))
    def _():
        o_ref[...] = jnp.zeros_like(o_ref)


def _combine_kernel(y0_ref, y1_ref, y2_ref, y3_ref, rw_ref, x_ref, gt_ref, gf_ref, o_ref, *, final):
    acc = jnp.zeros(o_ref.shape, F32)
    for k, y_ref in enumerate((y0_ref, y1_ref, y2_ref, y3_ref)):
        acc = acc + rw_ref[:, k:k + 1] * y_ref[...].astype(F32)
    out = x_ref[...] + gt_ref[...] * acc
    if final:
        out = out * lax.rsqrt(jnp.mean(out * out, axis=-1, keepdims=True) + EPS) * gf_ref[...]
    o_ref[...] = out


def _moe(x, g_ffn, mod, w_router, b_router, w_gu, b_gu, w_down, b_down, g_final, final, tiles):
    b, l, d = x.shape
    t = b * l
    tm, te = tiles["row"], tiles["expert"]
    n_exp = w_router.shape[1]
    dff = w_down.shape[1]
    assert TOP_K == 4
    wr = jnp.zeros((d, LANES), BF16).at[:, :n_exp].set(w_router.astype(BF16))
    br = jnp.full((1, LANES), NEG_BIG, F32).at[0, :n_exp].set(b_router)
    nt = l // tm
    h2, ri, rw, cnt = pl.pallas_call(
        functools.partial(_router_kernel, tm=tm),
        out_shape=(jax.ShapeDtypeStruct((b, l, d), BF16), jax.ShapeDtypeStruct((2 * TOP_K, t), jnp.int32),
                   jax.ShapeDtypeStruct((b, l, LANES), F32), jax.ShapeDtypeStruct((SUBLANES, LANES), jnp.int32)),
        grid=(b, nt),
        in_specs=[_row_spec(tm, d), _const_spec((1, d)), _mod_spec(d, 4), _mod_spec(d, 3),
                  _const_spec((d, LANES)), _const_spec((1, LANES))],
        out_specs=(_row_spec(tm, d), pl.BlockSpec((2 * TOP_K, tm), lambda bi, ti: (0, bi * nt + ti)),
                   _row_spec(tm, LANES), _const_spec((SUBLANES, LANES))),
        scratch_shapes=[pltpu.VMEM((SUBLANES, LANES), F32)],
        compiler_params=_params(("arbitrary", "arbitrary")),
        name="moe_router",
    )(x, g_ffn.reshape(1, d), mod, mod, wr, br)

    counts = cnt[0, :n_exp]
    padded = ((counts + te - 1) // te) * te
    ends = jnp.cumsum(padded)
    offs = ends - padded
    n_rows = t * TOP_K + n_exp * te
    n_tiles = n_rows // te
    tcol = min(t, 8192)
    dest = pl.pallas_call(
        functools.partial(_dest_kernel, n_exp=n_exp),
        out_shape=jax.ShapeDtypeStruct((2 * TOP_K, t), jnp.int32),
        grid_spec=pltpu.PrefetchScalarGridSpec(
            num_scalar_prefetch=1,
            grid=(t // tcol,),
            in_specs=[pl.BlockSpec((2 * TOP_K, tcol), lambda i, o: (0, i))],
            out_specs=pl.BlockSpec((2 * TOP_K, tcol), lambda i, o: (0, i))),
        compiler_params=_params(("parallel",)),
        name="moe_dest",
    )(offs.astype(jnp.int32), ri)[:TOP_K]
    slot = jnp.arange(te, dtype=jnp.int32)[None, :]
    n_pad = (padded - counts)[:, None]
    n_tail = te - n_pad
    tail_base = ends[-1] + (jnp.cumsum(n_tail[:, 0]) - n_tail[:, 0])[:, None]
    free_rows = jnp.where(slot < n_pad, (offs + counts)[:, None] + slot, tail_base + slot - n_pad)
    filler = (jnp.arange(n_exp * te, dtype=jnp.int32) * 61) % t
    keys = jnp.concatenate([dest.reshape(-1), free_rows.reshape(-1).astype(jnp.int32)])
    vals = jnp.concatenate([jnp.tile(jnp.arange(t, dtype=jnp.int32), TOP_K), filler])
    src = lax.sort((keys, vals), num_keys=1)[1]
    tile_start = jnp.arange(n_tiles, dtype=jnp.int32) * te
    tile_exp = jnp.minimum(jnp.sum(tile_start[:, None] >= ends[None, :], axis=1), n_exp - 1).astype(jnp.int32)
    n_valid = (ends[-1:] // te).astype(jnp.int32)
    xs = h2.reshape(t, d).at[src].get(mode="promise_in_bounds")

    ys = pl.pallas_call(
        functools.partial(_expert_kernel, dff=dff),
        out_shape=jax.ShapeDtypeStruct((n_rows, d), BF16),
        grid_spec=pltpu.PrefetchScalarGridSpec(
            num_scalar_prefetch=2,
            grid=(n_tiles,),
            in_specs=[pl.BlockSpec((te, d), lambda i, e, nv: (i, 0)),
                      pl.BlockSpec((None, d, 2 * dff), lambda i, e, nv: (e[i], 0, 0)),
                      pl.BlockSpec((None, 1, 2 * dff), lambda i, e, nv: (e[i], 0, 0)),
                      pl.BlockSpec((None, dff, d), lambda i, e, nv: (e[i], 0, 0)),
                      pl.BlockSpec((None, 1, d), lambda i, e, nv: (e[i], 0, 0))],
            out_specs=pl.BlockSpec((te, d), lambda i, e, nv: (i, 0)),
            scratch_shapes=[pltpu.VMEM((te, dff), BF16)]),
        compiler_params=_params(("arbitrary",)),
        name="moe_experts",
    )(tile_exp, n_valid, xs, w_gu, b_gu.reshape(n_exp, 1, 2 * dff), w_down, b_down.reshape(n_exp, 1, d))

    yk = [ys.at[dest[k]].get(mode="promise_in_bounds").reshape(b, l, d) for k in range(TOP_K)]
    return pl.pallas_call(
        functools.partial(_combine_kernel, final=final),
        out_shape=jax.ShapeDtypeStruct((b, l, d), F32),
        grid=(b, l // tm),
        in_specs=[_row_spec(tm, d)] * TOP_K + [_row_spec(tm, LANES), _row_spec(tm, d), _mod_spec(d, 5),
                                                _const_spec((1, d))],
        out_specs=_row_spec(tm, d),
        compiler_params=_params(("parallel", "parallel")),
        name="moe_combine",
    )(*yk, rw, x, mod, g_final.reshape(1, d))


def kernel(x, c, g_mix, g_ffn, w_ada, b_ada, gla_w_in, gla_w_gk1, gla_w_gk2, gla_b_gk, gla_g_onorm, gla_w_out, ml_w_up, ml_conv_w, ml_conv_b, ml_w_q, ml_w_k, ml_w_v, ml_w_if, ml_b_if, ml_skip, ml_g_norm, ml_w_down, s5_w_in, s5_a_re, s5_a_im, s5_log_dt, s5_b_re, s5_b_im, s5_c_re, s5_c_im, s5_d, s5_w_out, moe_w_router, moe_b_router, moe_w_gu, moe_b_gu, moe_w_down, moe_b_down, g_final):
    b, l, d = x.shape
    depth = w_ada.shape[0]
    n_exp = moe_w_router.shape[-1]
    assert l % CHUNK == 0 and d % MXU == 0
    n_parts = 2 if b % 2 == 0 else 1
    bp = b // n_parts
    tiles = _tiles(l, bp * l, n_exp)
    mods = _modulation(c, w_ada, b_ada)
    xs = [x[s * bp:(s + 1) * bp] for s in range(n_parts)]
    for i in range(depth):
        mp = [mods[i, s * bp:(s + 1) * bp].reshape(bp, 6, 1, d) for s in range(n_parts)]
        kind, j = i % N_MIXERS, i // N_MIXERS
        if kind == 0:
            xs = [_gla_mixer(xp, g_mix[i], mod, gla_w_in[j], gla_w_gk1[j], gla_w_gk2[j], gla_b_gk[j],
                             gla_g_onorm[j], gla_w_out[j], tiles) for xp, mod in zip(xs, mp)]
        elif kind == 1:
            xs = [_mlstm_mixer(xp, g_mix[i], mod, ml_w_up[j], ml_conv_w[j], ml_conv_b[j], ml_w_q[j], ml_w_k[j],
                               ml_w_v[j], ml_w_if[j], ml_b_if[j], ml_skip[j], ml_g_norm[j], ml_w_down[j], tiles)
                  for xp, mod in zip(xs, mp)]
        else:
            xs = _s5_mixer(xs, g_mix[i], mp, s5_w_in[j], s5_a_re[j], s5_a_im[j], s5_log_dt[j], s5_b_re[j],
                           s5_b_im[j], s5_c_re[j], s5_c_im[j], s5_d[j], s5_w_out[j], tiles)
        wgu, wdn = _cast_layer_bf16(moe_w_gu, i), _cast_layer_bf16(moe_w_down, i)
        xs = [_moe(xp, g_ffn[i], mod, moe_w_router[i], moe_b_router[i], wgu, moe_b_gu[i], wdn, moe_b_down[i],
                   g_final, i == depth - 1, tiles) for xp, mod in zip(xs, mp)]
    return jnp.concatenate(xs, axis=0)
```

```python
import functools

import jax
import jax.numpy as jnp
from jax import lax
from jax.experimental import pallas as pl
from jax.experimental.pallas import tpu as pltpu

F32 = jnp.float32
BF16 = jnp.bfloat16
HIGHEST = lax.Precision.HIGHEST

EPS = 1e-6
CHUNK = 64
N_MIXERS = 3
GLA_HEADS = 4
GLA_GATE_NORM = 16.0
MLSTM_HEADS = 4
S5_GROUP = 16
TOP_K = 4
SWIGLU_LIMIT = 7.0
SWIGLU_ALPHA = 1.702

LANES = 128
SUBLANES = 8
MXU = 256
VMEM_LIMIT_BYTES = 48 * 1024 * 1024

NEG_BIG = -1e30


def _tiles(seq, tokens, n_experts):
    row = min(1024, seq)
    gla = min(256, seq)
    gla_batch = 4
    mlstm = min(256, seq)
    s5 = min(64, seq)
    per_expert = tokens * TOP_K // n_experts
    expert = 512 if per_expert >= 2048 else 128
    return dict(row=row, gla=gla, gla_batch=gla_batch, mlstm=mlstm, s5=s5, expert=expert)


def _params(sem):
    return pltpu.CompilerParams(dimension_semantics=sem, vmem_limit_bytes=VMEM_LIMIT_BYTES)


def _normmod(x, g, sc, sh):
    y = x * lax.rsqrt(jnp.mean(x * x, axis=-1, keepdims=True) + EPS)
    return (y * g) * (1.0 + sc) + sh


def _log_sigmoid(x):
    return jnp.minimum(x, 0.0) - jnp.log(1.0 + jnp.exp(-jnp.abs(x)))


def _silu(x):
    return x * jax.nn.sigmoid(x)


def _mod_spec(d, j):
    return pl.BlockSpec((None, None, 1, d), lambda b, t: (b, j, 0, 0))


def _row_spec(tm, n):
    return pl.BlockSpec((None, tm, n), lambda b, t: (b, t, 0))


def _const_spec(shape):
    nd = len(shape)
    return pl.BlockSpec(shape, lambda b, t: (0,) * nd)


def _mod_kernel(c_ref, w_ref, b_ref, o_ref):
    ca = _silu(c_ref[...]).astype(BF16)
    o_ref[0] = jnp.dot(ca, w_ref[0].astype(BF16), preferred_element_type=F32) + b_ref[0]


def _modulation(c, w_ada, b_ada):
    depth, d, n = w_ada.shape
    b = c.shape[0]
    tn = n // 4
    return pl.pallas_call(
        _mod_kernel,
        out_shape=jax.ShapeDtypeStruct((depth, b, n), F32),
        grid=(depth, n // tn),
        in_specs=[pl.BlockSpec((b, d), lambda i, j: (0, 0)),
                  pl.BlockSpec((1, d, tn), lambda i, j: (i, 0, j)),
                  pl.BlockSpec((1, 1, tn), lambda i, j: (i, 0, j))],
        out_specs=pl.BlockSpec((1, b, tn), lambda i, j: (i, 0, j)),
        compiler_params=_params(("parallel", "parallel")),
        name="adaln_mod",
    )(c, w_ada, b_ada.reshape(depth, 1, n))


def _norm_proj_kernel(x_ref, g_ref, sc_ref, sh_ref, w_ref, o_ref):
    h = _normmod(x_ref[...], g_ref[...], sc_ref[...], sh_ref[...]).astype(BF16)
    o_ref[...] = jnp.dot(h, w_ref[...], preferred_element_type=F32).astype(o_ref.dtype)


def _norm_proj(x, g, mod, jsc, jsh, w, out_dtype, tm):
    b, l, d = x.shape
    n = w.shape[1]
    return pl.pallas_call(
        _norm_proj_kernel,
        out_shape=jax.ShapeDtypeStruct((b, l, n), out_dtype),
        grid=(b, l // tm),
        in_specs=[_row_spec(tm, d), _const_spec((1, d)), _mod_spec(d, jsc), _mod_spec(d, jsh),
                  _const_spec((d, n))],
        out_specs=_row_spec(tm, n),
        compiler_params=_params(("parallel", "parallel")),
        name="norm_proj",
    )(x, g.reshape(1, d), mod, mod, w)


def _proj_res_kernel(y_ref, w_ref, x_ref, gt_ref, o_ref):
    o_ref[...] = x_ref[...] + gt_ref[...] * jnp.dot(y_ref[...], w_ref[...], preferred_element_type=F32)


def _glu_res_kernel(y_ref, w_ref, x_ref, gt_ref, o_ref):
    d = o_ref.shape[-1]
    glu = jnp.dot(y_ref[...], w_ref[...], preferred_element_type=F32)
    o_ref[...] = x_ref[...] + gt_ref[...] * (glu[:, :d] * jax.nn.sigmoid(glu[:, d:]))


def _proj_res(kernel, y, w, x, mod, jgt, tm, name):
    b, l, d = x.shape
    k = y.shape[-1]
    n = w.shape[1]
    return pl.pallas_call(
        kernel,
        out_shape=jax.ShapeDtypeStruct((b, l, d), F32),
        grid=(b, l // tm),
        in_specs=[_row_spec(tm, k), _const_spec((k, n)), _row_spec(tm, d), _mod_spec(d, jgt)],
        out_specs=_row_spec(tm, d),
        compiler_params=_params(("parallel", "parallel")),
        name=name,
    )(y, w, x, mod)


def _gla_proj_kernel(x_ref, g_ref, sc_ref, sh_ref, w_ref, w1_ref, w2_ref, b2_ref, p_ref, la_ref, *, dk_total, qscale):
    h = _normmod(x_ref[...], g_ref[...], sc_ref[...], sh_ref[...]).astype(BF16)
    proj = jnp.dot(h, w_ref[...], preferred_element_type=F32)
    p_ref[:, :dk_total] = (proj[:, :dk_total] * qscale).astype(BF16)
    p_ref[:, dk_total:] = proj[:, dk_total:].astype(BF16)
    t1 = jnp.dot(h, w1_ref[...], preferred_element_type=F32).astype(BF16)
    gk = jnp.dot(t1, w2_ref[...], preferred_element_type=F32) + b2_ref[...]
    la_ref[...] = _log_sigmoid(gk) * (1.0 / GLA_GATE_NORM)


def _gla_scan_kernel(p_ref, la_ref, gon_ref, o_ref, st_ref, *, nb, tc, dk, dv):
    heads = GLA_HEADS
    dk_total, dv_total = heads * dk, heads * dv

    @pl.when(pl.program_id(1) == 0)
    def _():
        st_ref[...] = jnp.zeros_like(st_ref)

    rr = lax.broadcasted_iota(jnp.int32, (CHUNK, CHUNK), 0)
    cc = lax.broadcasted_iota(jnp.int32, (CHUNK, CHUNK), 1)
    tri = (rr >= cc).astype(F32)

    def chunk(j, carry):
        rows = pl.ds(pl.multiple_of(j * CHUNK, CHUNK), CHUNK)
        for bi in range(nb):
            cum_all = jnp.dot(tri, la_ref[bi, rows, :], preferred_element_type=F32, precision=HIGHEST)
            for h in range(heads):
                ks = slice(h * dk, (h + 1) * dk)
                vs = slice(h * dv, (h + 1) * dv)
                cum = cum_all[:, ks]
                tot = cum[CHUNK - 1:CHUNK, :]
                kblk = p_ref[bi, rows, dk_total + h * dk:dk_total + (h + 1) * dk].astype(F32)
                kd = (kblk * jnp.exp(tot - cum)).astype(BF16)
                v = p_ref[bi, rows, 2 * dk_total + h * dv:2 * dk_total + (h + 1) * dv]
                upd = lax.dot_general(v, kd, (((0,), (0,)), ((), ())), preferred_element_type=F32)
                st = jnp.exp(tot) * st_ref[bi, h] + upd
                st_ref[bi, h] = st
                q = p_ref[bi, rows, ks]
                o = lax.dot_general(q, st.astype(BF16), (((1,), (1,)), ((), ())), preferred_element_type=F32)
                o = o * lax.rsqrt(jnp.mean(o * o, axis=-1, keepdims=True) + EPS) * gon_ref[...]
                g0 = 2 * dk_total + dv_total + h * dv
                g = p_ref[bi, rows, g0:g0 + dv].astype(F32)
                o_ref[bi, rows, vs] = (o * _silu(g)).astype(BF16)
        return carry

    lax.fori_loop(0, tc // CHUNK, chunk, 0)


def _gla_mixer(x, g_mix, mod, w_in, w_gk1, w_gk2, b_gk, g_onorm, w_out, tiles):
    b, l, d = x.shape
    tm, tc = tiles["row"], tiles["gla"]
    rank = w_gk1.shape[1]
    dk_total = w_gk2.shape[1]
    dv_total = w_out.shape[0]
    dk, dv = dk_total // GLA_HEADS, dv_total // GLA_HEADS
    n = w_in.shape[1]
    w1 = jnp.zeros((d, LANES), BF16).at[:, :rank].set(w_gk1.astype(BF16))
    w2 = jnp.zeros((LANES, dk_total), BF16).at[:rank, :].set(w_gk2.astype(BF16))
    p, la = pl.pallas_call(
        functools.partial(_gla_proj_kernel, dk_total=dk_total, qscale=dk ** -0.5),
        out_shape=(jax.ShapeDtypeStruct((b, l, n), BF16), jax.ShapeDtypeStruct((b, l, dk_total), F32)),
        grid=(b, l // tm),
        in_specs=[_row_spec(tm, d), _const_spec((1, d)), _mod_spec(d, 1), _mod_spec(d, 0),
                  _const_spec((d, n)), _const_spec((d, LANES)), _const_spec((LANES, dk_total)),
                  _const_spec((1, dk_total))],
        out_specs=(_row_spec(tm, n), _row_spec(tm, dk_total)),
        compiler_params=_params(("parallel", "parallel")),
        name="gla_proj",
    )(x, g_mix.reshape(1, d), mod, mod, w_in.astype(BF16), w1, w2, b_gk.reshape(1, dk_total))
    nb = tiles["gla_batch"] if b % tiles["gla_batch"] == 0 else 1

    def scan_spec(width):
        return pl.BlockSpec((nb, tc, width), lambda bi, ti: (bi, ti, 0))

    o = pl.pallas_call(
        functools.partial(_gla_scan_kernel, nb=nb, tc=tc, dk=dk, dv=dv),
        out_shape=jax.ShapeDtypeStruct((b, l, dv_total), BF16),
        grid=(b // nb, l // tc),
        in_specs=[scan_spec(n), scan_spec(dk_total), _const_spec((1, dv))],
        out_specs=scan_spec(dv_total),
        scratch_shapes=[pltpu.VMEM((nb, GLA_HEADS, dv, dk), F32)],
        compiler_params=_params(("arbitrary", "arbitrary")),
        name="gla_scan",
    )(p, la, g_onorm.reshape(1, dv))
    return _proj_res(_proj_res_kernel, o, w_out.astype(BF16), x, mod, 2, tm, "gla_out")


def _mlstm_kernel(xz_ref, cw_ref, cb_ref, wq_ref, wk_ref, wv_ref, wif_ref, bif_ref, skip_ref, gn_ref,
                  o_ref, ext_ref, q_ref, k_ref, v_ref, xc_ref, c_ref, n_ref, m_ref, *, tc, inner):
    heads = MLSTM_HEADS
    dh = inner // heads
    conv_w = cw_ref.shape[0]
    halo = SUBLANES

    @pl.when(pl.program_id(1) == 0)
    def _():
        ext_ref[0:halo, :] = jnp.zeros((halo, inner), F32)
        c_ref[...] = jnp.zeros_like(c_ref)
        n_ref[...] = jnp.zeros_like(n_ref)
        m_ref[...] = jnp.zeros_like(m_ref)

    ext_ref[halo:halo + tc, :] = xz_ref[:, :inner].astype(F32)
    acc = jnp.broadcast_to(cb_ref[...], (tc, inner))
    for j in range(conv_w):
        off = halo - (conv_w - 1) + j
        acc = acc + cw_ref[j:j + 1, :] * ext_ref[off:off + tc, :]
    ext_ref[0:halo, :] = ext_ref[tc:tc + halo, :]
    xc = _silu(acc)
    xc_ref[...] = xc

    gates = jnp.broadcast_to(bif_ref[...], (tc, LANES))
    for i in range(inner // MXU):
        sl = slice(i * MXU, (i + 1) * MXU)
        xcb = xc[:, sl].astype(BF16)
        q = jnp.dot(xcb, wq_ref[i], preferred_element_type=F32)
        k = jnp.dot(xcb, wk_ref[i], preferred_element_type=F32)
        v = jnp.dot(xz_ref[:, sl], wv_ref[i], preferred_element_type=F32)
        qb, kb, vb = q.astype(BF16), k.astype(BF16), v.astype(BF16)
        q_ref[:, sl] = qb
        k_ref[:, sl] = (k * dh ** -0.5).astype(BF16)
        v_ref[:, sl] = vb
        gates = gates + jnp.dot(qb, wif_ref[i * MXU:(i + 1) * MXU, :], preferred_element_type=F32)
        gates = gates + jnp.dot(kb, wif_ref[inner + i * MXU:inner + (i + 1) * MXU, :], preferred_element_type=F32)
        gates = gates + jnp.dot(vb, wif_ref[2 * inner + i * MXU:2 * inner + (i + 1) * MXU, :],
                                preferred_element_type=F32)
    rr = lax.broadcasted_iota(jnp.int32, (tc, tc), 0)
    cc = lax.broadcasted_iota(jnp.int32, (tc, tc), 1)
    cum = jnp.dot((rr >= cc).astype(F32), _log_sigmoid(gates), preferred_element_type=F32, precision=HIGHEST)
    a_all = gates - pltpu.roll(cum, LANES - heads, 1)
    a_rows = a_all.T
    chunk_r = lax.broadcasted_iota(jnp.int32, (tc, 1), 0) // CHUNK
    chunk_c = lax.broadcasted_iota(jnp.int32, (1, tc), 1) // CHUNK
    visible = chunk_c <= chunk_r
    n_chunks = tc // CHUNK
    for h in range(heads):
        hs = slice(h * dh, (h + 1) * dh)
        a_col = a_all[:, h:h + 1]
        f_col = cum[:, heads + h:heads + h + 1]
        m_in = m_ref[h][:, 0:1]
        mu = m_in
        mu_col = jnp.zeros((tc, 1), F32)
        f_end = jnp.zeros((tc, 1), F32)
        for c in range(n_chunks):
            mu = jnp.maximum(mu, jnp.max(jnp.where(chunk_r == c, a_col, NEG_BIG), axis=0, keepdims=True))
            mu_col = jnp.where(chunk_r == c, mu, mu_col)
            f_end = jnp.where(chunk_r == c, f_col[(c + 1) * CHUNK - 1:(c + 1) * CHUNK, :], f_end)
        q = q_ref[:, hs]
        k = k_ref[:, hs]
        v = v_ref[:, hs]
        w = jnp.where(visible, jnp.exp(a_rows[h:h + 1, :] - mu_col), 0.0)
        p = lax.dot_general(q, k, (((1,), (1,)), ((), ())), preferred_element_type=F32) * w
        carry_in = jnp.exp(m_in - mu_col)
        num = jnp.dot(p.astype(BF16), v, preferred_element_type=F32) \
            + carry_in * jnp.dot(q, c_ref[h].astype(BF16), preferred_element_type=F32)
        qn = carry_in * jnp.sum(q.astype(F32) * n_ref[h], axis=-1, keepdims=True) + jnp.sum(p, axis=-1, keepdims=True)
        den = jnp.maximum(jnp.abs(qn), jnp.exp(-(mu_col + f_end)))
        hc = num / den
        mean = jnp.mean(hc, axis=-1, keepdims=True)
        var = jnp.mean((hc - mean) * (hc - mean), axis=-1, keepdims=True)
        hn = (hc - mean) * lax.rsqrt(var + EPS) * gn_ref[:, hs]
        z = xz_ref[:, inner + h * dh:inner + (h + 1) * dh].astype(F32)
        o_ref[:, hs] = ((hn + skip_ref[:, hs] * xc_ref[:, hs]) * _silu(z)).astype(BF16)
        decay = jnp.exp(m_in - mu)
        kw = k.astype(F32) * jnp.exp(a_col - mu)
        c_ref[h] = decay * c_ref[h] + lax.dot_general(kw.astype(BF16), v, (((0,), (0,)), ((), ())),
                                                      preferred_element_type=F32)
        n_ref[h] = decay * n_ref[h] + jnp.sum(kw, axis=0, keepdims=True)
        m_ref[h] = jnp.broadcast_to(mu + f_col[tc - 1:tc, :], (1, LANES))


def _diag_tiles(w):
    nblk, bs, _ = w.shape
    per = MXU // bs
    wt = w.reshape(nblk // per, per, bs, bs)
    eye = jnp.eye(per, dtype=w.dtype)
    return jnp.einsum("tpio,pq->tpiqo", wt, eye).reshape(nblk // per, MXU, MXU).astype(BF16)


def _mlstm_mixer(x, g_mix, mod, w_up, conv_w, conv_b, w_q, w_k, w_v, w_if, b_if, skip, g_norm, w_down, tiles):
    b, l, d = x.shape
    tm, tc = tiles["row"], tiles["mlstm"]
    inner = w_down.shape[0]
    heads = MLSTM_HEADS
    xz = _norm_proj(x, g_mix, mod, 1, 0, w_up.astype(BF16), BF16, tm)
    wif = jnp.zeros((3 * inner, LANES), BF16).at[:, :2 * heads].set(w_if.astype(BF16))
    bif = jnp.zeros((1, LANES), F32).at[0, :2 * heads].set(b_if)
    nt = inner // MXU
    y = pl.pallas_call(
        functools.partial(_mlstm_kernel, tc=tc, inner=inner),
        out_shape=jax.ShapeDtypeStruct((b, l, inner), BF16),
        grid=(b, l // tc),
        in_specs=[_row_spec(tc, 2 * inner), _const_spec(conv_w.shape), _const_spec((1, inner)),
                  _const_spec((nt, MXU, MXU)), _const_spec((nt, MXU, MXU)), _const_spec((nt, MXU, MXU)),
                  _const_spec((3 * inner, LANES)), _const_spec((1, LANES)), _const_spec((1, inner)),
                  _const_spec((1, inner))],
        out_specs=_row_spec(tc, inner),
        scratch_shapes=[pltpu.VMEM((tc + 2 * SUBLANES, inner), F32),
                        pltpu.VMEM((tc, inner), BF16), pltpu.VMEM((tc, inner), BF16),
                        pltpu.VMEM((tc, inner), BF16), pltpu.VMEM((tc, inner), F32),
                        pltpu.VMEM((heads, inner // heads, inner // heads), F32),
                        pltpu.VMEM((heads, 1, inner // heads), F32),
                        pltpu.VMEM((heads, 1, LANES), F32)],
        compiler_params=_params(("arbitrary", "arbitrary")),
        name="mlstm_scan",
    )(xz, conv_w, conv_b.reshape(1, inner), _diag_tiles(w_q), _diag_tiles(w_k), _diag_tiles(w_v), wif, bif,
      skip.reshape(1, inner), g_norm.reshape(1, inner))
    return _proj_res(_proj_res_kernel, y, w_down.astype(BF16), x, mod, 2, tm, "mlstm_out")


def _s5_kernel(u_ref, bt_ref, ct_ref, are_ref, aim_ref, d_ref, y_ref, bu_ref, st_ref, *, ts, nb, d, npart, per):
    rows = ts * nb
    half = per // 2
    ntile = npart // MXU

    @pl.when(pl.program_id(0) == 0)
    def _():
        st_ref[...] = jnp.zeros_like(st_ref)

    def col_tile(t, j):
        return half * t + j if j < half else ntile + half * t + (j - half)

    u = u_ref[...].reshape(rows, d)
    ub = u.astype(BF16)
    for kt in range(d // MXU):
        lhs = ub[:, kt * MXU:(kt + 1) * MXU]
        for j in range(per):
            n = col_tile(kt, j)
            bu_ref[:, n * MXU:(n + 1) * MXU] = jnp.dot(lhs, bt_ref[kt * per + j], preferred_element_type=F32)

    cb = min(512, npart)
    for c in range(npart // cb):
        re = slice(c * cb, (c + 1) * cb)
        im = slice(npart + c * cb, npart + (c + 1) * cb)
        ar = are_ref[:, re]
        ai = aim_ref[:, re]

        def step(t, carry):
            xr, xi = carry
            r = pl.ds(pl.multiple_of(t * nb, nb), nb)
            nr = ar * xr - ai * xi + bu_ref[r, re]
            ni = ar * xi + ai * xr + bu_ref[r, im]
            bu_ref[r, re] = nr
            bu_ref[r, im] = ni
            return nr, ni

        xr, xi = lax.fori_loop(0, ts, step, (st_ref[:, re], st_ref[:, im]), unroll=4)
        st_ref[:, re] = xr
        st_ref[:, im] = xi

    for nt in range(d // MXU):
        acc = jnp.zeros((rows, MXU), F32)
        for j in range(per):
            n = col_tile(nt, j)
            acc = acc + jnp.dot(bu_ref[:, n * MXU:(n + 1) * MXU].astype(BF16), ct_ref[nt * per + j],
                                preferred_element_type=F32)
        ch = slice(nt * MXU, (nt + 1) * MXU)
        yv = acc + d_ref[:, ch] * u[:, ch]
        y_ref[:, :, ch] = jax.nn.gelu(yv).reshape(ts, nb, MXU)


def _s5_mixer(xs, g_mix, mods, w_in, a_re, a_im, log_dt, b_re, b_im, c_re, c_im, d_skip, w_out, tiles):
    l, d = xs[0].shape[1:]
    b = sum(p.shape[0] for p in xs)
    tm, ts = tiles["row"], tiles["s5"]
    groups, pst = a_re.shape
    npart = groups * pst
    gk = MXU // S5_GROUP
    gn = MXU // pst
    per = 2 * (gk // gn)
    nkt = d // MXU

    a = lax.complex(a_re, a_im)
    dt = jnp.exp(log_dt)[:, None]
    a_bar = jnp.exp(a * dt)
    b_bar = ((a_bar - 1.0) / a)[..., None] * lax.complex(b_re, b_im)
    eye = jnp.eye(groups, dtype=F32)
    bfull = jnp.concatenate([jnp.einsum("gpc,gh->gchp", part, eye).reshape(d, npart)
                             for part in (jnp.real(b_bar), jnp.imag(b_bar))], axis=1)
    cfull = jnp.concatenate([jnp.einsum("gcp,gh->gphc", part, eye).reshape(npart, d)
                             for part in (c_re, -c_im)], axis=0)
    half = per // 2
    ntile = npart // MXU

    def col_tile(t, j):
        return half * t + j if j < half else ntile + half * t + (j - half)

    bt = jnp.stack([bfull[kt * MXU:(kt + 1) * MXU, col_tile(kt, j) * MXU:(col_tile(kt, j) + 1) * MXU]
                    for kt in range(nkt) for j in range(per)]).astype(BF16)
    ct = jnp.stack([cfull[col_tile(nt, j) * MXU:(col_tile(nt, j) + 1) * MXU, nt * MXU:(nt + 1) * MXU]
                    for nt in range(nkt) for j in range(per)]).astype(BF16)
    are = jnp.broadcast_to(jnp.real(a_bar).reshape(1, npart), (b, npart))
    aim = jnp.broadcast_to(jnp.imag(a_bar).reshape(1, npart), (b, npart))

    win = w_in.astype(BF16)
    u = jnp.concatenate([_norm_proj(x, g_mix, mod, 1, 0, win, F32, tm) for x, mod in zip(xs, mods)], axis=0)
    ut = jnp.transpose(u, (1, 0, 2))
    yt = pl.pallas_call(
        functools.partial(_s5_kernel, ts=ts, nb=b, d=d, npart=npart, per=per),
        out_shape=jax.ShapeDtypeStruct((l, b, d), F32),
        grid=(l // ts,),
        in_specs=[pl.BlockSpec((ts, b, d), lambda t: (t, 0, 0)),
                  pl.BlockSpec((nkt * per, MXU, MXU), lambda t: (0, 0, 0)),
                  pl.BlockSpec((nkt * per, MXU, MXU), lambda t: (0, 0, 0)),
                  pl.BlockSpec((b, npart), lambda t: (0, 0)),
                  pl.BlockSpec((b, npart), lambda t: (0, 0)),
                  pl.BlockSpec((1, d), lambda t: (0, 0))],
        out_specs=pl.BlockSpec((ts, b, d), lambda t: (t, 0, 0)),
        scratch_shapes=[pltpu.VMEM((ts * b, 2 * npart), F32), pltpu.VMEM((b, 2 * npart), F32)],
        compiler_params=_params(("arbitrary",)),
        name="s5_scan",
    )(ut, bt, ct, are, aim, d_skip.reshape(1, d))
    y = jnp.transpose(yt, (1, 0, 2)).astype(BF16)
    wout = w_out.astype(BF16)
    outs, b0 = [], 0
    for x, mod in zip(xs, mods):
        outs.append(_proj_res(_glu_res_kernel, y[b0:b0 + x.shape[0]], wout, x, mod, 2, tm, "s5_out"))
        b0 += x.shape[0]
    return outs


def _router_kernel(x_ref, g_ref, sc_ref, sh_ref, wr_ref, br_ref, h_ref, ri_ref, rw_ref, cnt_ref, run_ref, *, tm):
    @pl.when((pl.program_id(0) == 0) & (pl.program_id(1) == 0))
    def _():
        run_ref[...] = jnp.zeros_like(run_ref)

    hb = _normmod(x_ref[...], g_ref[...], sc_ref[...], sh_ref[...]).astype(BF16)
    h_ref[...] = hb
    logits = jnp.dot(hb, wr_ref[...], preferred_element_type=F32) + br_ref[...]
    lane = lax.broadcasted_iota(jnp.int32, logits.shape, 1)
    lane_f = lane.astype(F32)
    work = logits
    vals, idxs = [], []
    for _ in range(TOP_K):
        m = jnp.max(work, axis=-1, keepdims=True)
        idx = jnp.min(jnp.where(work == m, lane_f, float(LANES)), axis=-1, keepdims=True)
        vals.append(m)
        idxs.append(idx)
        work = jnp.where(lane_f == idx, 2.0 * NEG_BIG, work)
    exps = [jnp.exp(v - vals[0]) for v in vals]
    denom = exps[0]
    for e in exps[1:]:
        denom = denom + e
    multi = jnp.zeros(logits.shape, F32)
    for idx in idxs:
        multi = multi + (lane_f == idx).astype(F32)
    rr = lax.broadcasted_iota(jnp.int32, (tm, tm), 0)
    cc = lax.broadcasted_iota(jnp.int32, (tm, tm), 1)
    before = (rr > cc).astype(BF16)
    cs = jnp.dot(before, multi.astype(BF16), preferred_element_type=F32) + run_ref[0:1, :]
    run = run_ref[0:1, :] + jnp.sum(multi, axis=0, keepdims=True)
    run_ref[...] = jnp.broadcast_to(run, run_ref.shape)
    cnt_ref[...] = jnp.broadcast_to(run, cnt_ref.shape).astype(jnp.int32)
    ri = jnp.zeros(logits.shape, F32)
    rw = jnp.zeros(logits.shape, F32)
    for k in range(TOP_K):
        rank = jnp.sum(jnp.where(lane_f == idxs[k], cs, 0.0), axis=-1, keepdims=True)
        ri = ri + jnp.where(lane == k, idxs[k], 0.0) + jnp.where(lane == TOP_K + k, rank, 0.0)
        rw = rw + jnp.where(lane == k, exps[k] / denom, 0.0)
    ri_ref[...] = ri.T[:2 * TOP_K, :].astype(jnp.int32)
    rw_ref[...] = rw


def _dest_kernel(offs_ref, ri_ref, o_ref, *, n_exp):
    idx = ri_ref[0:TOP_K, :]
    off = jnp.zeros(idx.shape, jnp.int32)
    for e in range(n_exp):
        off = jnp.where(idx == e, offs_ref[e], off)
    o_ref[0:TOP_K, :] = off + ri_ref[TOP_K:2 * TOP_K, :]
    o_ref[TOP_K:2 * TOP_K, :] = idx


def _cast_kernel(w_ref, o_ref):
    o_ref[...] = w_ref[...].astype(o_ref.dtype)


def _cast_layer_bf16(w, layer):
    _, n_exp, r, c = w.shape
    return pl.pallas_call(
        _cast_kernel,
        out_shape=jax.ShapeDtypeStruct((n_exp, r, c), BF16),
        grid=(n_exp,),
        in_specs=[pl.BlockSpec((None, None, r, c), lambda e: (layer, e, 0, 0))],
        out_specs=pl.BlockSpec((None, r, c), lambda e: (e, 0, 0)),
        compiler_params=_params(("parallel",)),
        name="moe_weight_cast",
    )(w)


def _expert_kernel(te_ref, nv_ref, xs_ref, wgu_ref, bgu_ref, wd_ref, bd_ref, o_ref, act_ref, *, dff):
    valid = pl.program_id(0) < nv_ref[0]

    @pl.when(valid)
    def _():
        x = xs_ref[...]
        gate = jnp.dot(x, wgu_ref[:, :dff], preferred_element_type=F32) + bgu_ref[:, :dff]
        up = jnp.dot(x, wgu_ref[:, dff:], preferred_element_type=F32) + bgu_ref[:, dff:]
        gate = jnp.minimum(gate, SWIGLU_LIMIT)
        up = jnp.clip(up, -SWIGLU_LIMIT, SWIGLU_LIMIT)
        act_ref[...] = ((up + 1.0) * gate * jax.nn.sigmoid(SWIGLU_ALPHA * gate)).astype(BF16)
        o_ref[...] = (jnp.dot(act_ref[...], wd_ref[...], preferred_element_type=F32) + bd_ref[...]).astype(o_ref.dtype)

    @pl.when(jnp.logical_not(valid))
    def _():
        o_ref[...] = jnp.zeros_like(o_ref)


def _combine_kernel(y0_ref, y1_ref, y2_ref, y3_ref, rw_ref, x_ref, gt_ref, gf_ref, o_ref, *, final):
    acc = jnp.zeros(o_ref.shape, F32)
    for k, y_ref in enumerate((y0_ref, y1_ref, y2_ref, y3_ref)):
        acc = acc + rw_ref[:, k:k + 1] * y_ref[...].astype(F32)
    out = x_ref[...] + gt_ref[...] * acc
    if final:
        out = out * lax.rsqrt(jnp.mean(out * out, axis=-1, keepdims=True) + EPS) * gf_ref[...]
    o_ref[...] = out


def _moe(x, g_ffn, mod, w_router, b_router, w_gu, b_gu, w_down, b_down, g_final, final, tiles):
    b, l, d = x.shape
    t = b * l
    tm, te = tiles["row"], tiles["expert"]
    n_exp = w_router.shape[1]
    dff = w_down.shape[1]
    assert TOP_K == 4
    wr = jnp.zeros((d, LANES), BF16).at[:, :n_exp].set(w_router.astype(BF16))
    br = jnp.full((1, LANES), NEG_BIG, F32).at[0, :n_exp].set(b_router)
    nt = l // tm
    h2, ri, rw, cnt = pl.pallas_call(
        functools.partial(_router_kernel, tm=tm),
        out_shape=(jax.ShapeDtypeStruct((b, l, d), BF16), jax.ShapeDtypeStruct((2 * TOP_K, t), jnp.int32),
                   jax.ShapeDtypeStruct((b, l, LANES), F32), jax.ShapeDtypeStruct((SUBLANES, LANES), jnp.int32)),
        grid=(b, nt),
        in_specs=[_row_spec(tm, d), _const_spec((1, d)), _mod_spec(d, 4), _mod_spec(d, 3),
                  _const_spec((d, LANES)), _const_spec((1, LANES))],
        out_specs=(_row_spec(tm, d), pl.BlockSpec((2 * TOP_K, tm), lambda bi, ti: (0, bi * nt + ti)),
                   _row_spec(tm, LANES), _const_spec((SUBLANES, LANES))),
        scratch_shapes=[pltpu.VMEM((SUBLANES, LANES), F32)],
        compiler_params=_params(("arbitrary", "arbitrary")),
        name="moe_router",
    )(x, g_ffn.reshape(1, d), mod, mod, wr, br)

    counts = cnt[0, :n_exp]
    padded = ((counts + te - 1) // te) * te
    ends = jnp.cumsum(padded)
    offs = ends - padded
    n_rows = t * TOP_K + n_exp * te
    n_tiles = n_rows // te
    tcol = min(t, 8192)
    dest = pl.pallas_call(
        functools.partial(_dest_kernel, n_exp=n_exp),
        out_shape=jax.ShapeDtypeStruct((2 * TOP_K, t), jnp.int32),
        grid_spec=pltpu.PrefetchScalarGridSpec(
            num_scalar_prefetch=1,
            grid=(t // tcol,),
            in_specs=[pl.BlockSpec((2 * TOP_K, tcol), lambda i, o: (0, i))],
            out_specs=pl.BlockSpec((2 * TOP_K, tcol), lambda i, o: (0, i))),
        compiler_params=_params(("parallel",)),
        name="moe_dest",
    )(offs.astype(jnp.int32), ri)[:TOP_K]
    slot = jnp.arange(te, dtype=jnp.int32)[None, :]
    n_pad = (padded - counts)[:, None]
    n_tail = te - n_pad
    tail_base = ends[-1] + (jnp.cumsum(n_tail[:, 0]) - n_tail[:, 0])[:, None]
    free_rows = jnp.where(slot < n_pad, (offs + counts)[:, None] + slot, tail_base + slot - n_pad)
    filler = (jnp.arange(n_exp * te, dtype=jnp.int32) * 61) % t
    keys = jnp.concatenate([dest.reshape(-1), free_rows.reshape(-1).astype(jnp.int32)])
    vals = jnp.concatenate([jnp.tile(jnp.arange(t, dtype=jnp.int32), TOP_K), filler])
    src = lax.sort((keys, vals), num_keys=1)[1]
    tile_start = jnp.arange(n_tiles, dtype=jnp.int32) * te
    tile_exp = jnp.minimum(jnp.sum(tile_start[:, None] >= ends[None, :], axis=1), n_exp - 1).astype(jnp.int32)
    n_valid = (ends[-1:] // te).astype(jnp.int32)
    xs = h2.reshape(t, d).at[src].get(mode="promise_in_bounds")

    ys = pl.pallas_call(
        functools.partial(_expert_kernel, dff=dff),
        out_shape=jax.ShapeDtypeStruct((n_rows, d), BF16),
        grid_spec=pltpu.PrefetchScalarGridSpec(
            num_scalar_prefetch=2,
            grid=(n_tiles,),
            in_specs=[pl.BlockSpec((te, d), lambda i, e, nv: (i, 0)),
                      pl.BlockSpec((None, d, 2 * dff), lambda i, e, nv: (e[i], 0, 0)),
                      pl.BlockSpec((None, 1, 2 * dff), lambda i, e, nv: (e[i], 0, 0)),
                      pl.BlockSpec((None, dff, d), lambda i, e, nv: (e[i], 0, 0)),
                      pl.BlockSpec((None, 1, d), lambda i, e, nv: (e[i], 0, 0))],
            out_specs=pl.BlockSpec((te, d), lambda i, e, nv: (i, 0)),
            scratch_shapes=[pltpu.VMEM((te, dff), BF16)]),
        compiler_params=_params(("arbitrary",)),
        name="moe_experts",
    )(tile_exp, n_valid, xs, w_gu, b_gu.reshape(n_exp, 1, 2 * dff), w_down, b_down.reshape(n_exp, 1, d))

    yk = [ys.at[dest[k]].get(mode="promise_in_bounds").reshape(b, l, d) for k in range(TOP_K)]
    return pl.pallas_call(
        functools.partial(_combine_kernel, final=final),
        out_shape=jax.ShapeDtypeStruct((b, l, d), F32),
        grid=(b, l // tm),
        in_specs=[_row_spec(tm, d)] * TOP_K + [_row_spec(tm, LANES), _row_spec(tm, d), _mod_spec(d, 5),
                                                _const_spec((1, d))],
        out_specs=_row_spec(tm, d),
        compiler_params=_params(("parallel", "parallel")),
        name="moe_combine",
    )(*yk, rw, x, mod, g_final.reshape(1, d))


def kernel(x, c, g_mix, g_ffn, w_ada, b_ada, gla_w_in, gla_w_gk1, gla_w_gk2, gla_b_gk, gla_g_onorm, gla_w_out, ml_w_up, ml_conv_w, ml_conv_b, ml_w_q, ml_w_k, ml_w_v, ml_w_if, ml_b_if, ml_skip, ml_g_norm, ml_w_down, s5_w_in, s5_a_re, s5_a_im, s5_log_dt, s5_b_re, s5_b_im, s5_c_re, s5_c_im, s5_d, s5_w_out, moe_w_router, moe_b_router, moe_w_gu, moe_b_gu, moe_w_down, moe_b_down, g_final):
    b, l, d = x.shape
    depth = w_ada.shape[0]
    n_exp = moe_w_router.shape[-1]
    assert l % CHUNK == 0 and d % MXU == 0
    n_parts = 2 if b % 2 == 0 else 1
    bp = b // n_parts
    tiles = _tiles(l, bp * l, n_exp)
    mods = _modulation(c, w_ada, b_ada)
    xs = [x[s * bp:(s + 1) * bp] for s in range(n_parts)]
    for i in range(depth):
        mp = [mods[i, s * bp:(s + 1) * bp].reshape(bp, 6, 1, d) for s in range(n_parts)]
        kind, j = i % N_MIXERS, i // N_MIXERS
        if kind == 0:
            xs = [_gla_mixer(xp, g_mix[i], mod, gla_w_in[j], gla_w_gk1[j], gla_w_gk2[j], gla_b_gk[j],
                             gla_g_onorm[j], gla_w_out[j], tiles) for xp, mod in zip(xs, mp)]
        elif kind == 1:
            xs = [_mlstm_mixer(xp, g_mix[i], mod, ml_w_up[j], ml_conv_w[j], ml_conv_b[j], ml_w_q[j], ml_w_k[j],
                               ml_w_v[j], ml_w_if[j], ml_b_if[j], ml_skip[j], ml_g_norm[j], ml_w_down[j], tiles)
                  for xp, mod in zip(xs, mp)]
        else:
            xs = _s5_mixer(xs, g_mix[i], mp, s5_w_in[j], s5_a_re[j], s5_a_im[j], s5_log_dt[j], s5_b_re[j],
                           s5_b_im[j], s5_c_re[j], s5_c_im[j], s5_d[j], s5_w_out[j], tiles)
        wgu, wdn = _cast_layer_bf16(moe_w_gu, i), _cast_layer_bf16(moe_w_down, i)
        xs = [_moe(xp, g_ffn[i], mod, moe_w_router[i], moe_b_router[i], wgu, moe_b_gu[i], wdn, moe_b_down[i],
                   g_final, i == depth - 1, tiles) for xp, mod in zip(xs, mp)]
    return jnp.concatenate(xs, axis=0)
```

```python
import functools

import jax
import jax.numpy as jnp
from jax import lax
from jax.experimental import pallas as pl
from jax.experimental.pallas import tpu as pltpu

F32 = jnp.float32
BF16 = jnp.bfloat16
HIGHEST = lax.Precision.HIGHEST

EPS = 1e-6
CHUNK = 64
N_MIXERS = 3
GLA_HEADS = 4
GLA_GATE_NORM = 16.0
MLSTM_HEADS = 4
S5_GROUP = 16
TOP_K = 4
SWIGLU_LIMIT = 7.0
SWIGLU_ALPHA = 1.702

LANES = 128
SUBLANES = 8
MXU = 256
VMEM_LIMIT_BYTES = 56 * 1024 * 1024

NEG_BIG = -1e30


def _tiles(seq, tokens, n_experts):
    row = min(1024, seq)
    gla = min(256, seq)
    gla_batch = 4
    mlstm = min(256, seq)
    s5 = min(64, seq)
    per_expert = tokens * TOP_K // n_experts
    expert = 512 if per_expert >= 2048 else 128
    return dict(row=row, gla=gla, gla_batch=gla_batch, mlstm=mlstm, s5=s5, expert=expert)


def _params(sem):
    return pltpu.CompilerParams(dimension_semantics=sem, vmem_limit_bytes=VMEM_LIMIT_BYTES)


def _normmod(x, g, sc, sh):
    y = x * lax.rsqrt(jnp.mean(x * x, axis=-1, keepdims=True) + EPS)
    return (y * g) * (1.0 + sc) + sh


def _log_sigmoid(x):
    return jnp.minimum(x, 0.0) - jnp.log(1.0 + jnp.exp(-jnp.abs(x)))


def _silu(x):
    return x * jax.nn.sigmoid(x)


def _mod_spec(d, j):
    return pl.BlockSpec((None, None, 1, d), lambda b, t: (b, j, 0, 0))


def _row_spec(tm, n):
    return pl.BlockSpec((None, tm, n), lambda b, t: (b, t, 0))


def _const_spec(shape):
    nd = len(shape)
    return pl.BlockSpec(shape, lambda b, t: (0,) * nd)


def _mod_kernel(c_ref, w_ref, b_ref, o_ref):
    ca = _silu(c_ref[...]).astype(BF16)
    o_ref[0] = jnp.dot(ca, w_ref[0].astype(BF16), preferred_element_type=F32) + b_ref[0]


def _modulation(c, w_ada, b_ada):
    depth, d, n = w_ada.shape
    b = c.shape[0]
    tn = n // 4
    return pl.pallas_call(
        _mod_kernel,
        out_shape=jax.ShapeDtypeStruct((depth, b, n), F32),
        grid=(depth, n // tn),
        in_specs=[pl.BlockSpec((b, d), lambda i, j: (0, 0)),
                  pl.BlockSpec((1, d, tn), lambda i, j: (i, 0, j)),
                  pl.BlockSpec((1, 1, tn), lambda i, j: (i, 0, j))],
        out_specs=pl.BlockSpec((1, b, tn), lambda i, j: (i, 0, j)),
        compiler_params=_params(("parallel", "parallel")),
        name="adaln_mod",
    )(c, w_ada, b_ada.reshape(depth, 1, n))


def _norm_proj_kernel(x_ref, g_ref, sc_ref, sh_ref, w_ref, o_ref):
    h = _normmod(x_ref[...], g_ref[...], sc_ref[...], sh_ref[...]).astype(BF16)
    o_ref[...] = jnp.dot(h, w_ref[...], preferred_element_type=F32).astype(o_ref.dtype)


def _norm_proj(x, g, mod, jsc, jsh, w, out_dtype, tm):
    b, l, d = x.shape
    n = w.shape[1]
    return pl.pallas_call(
        _norm_proj_kernel,
        out_shape=jax.ShapeDtypeStruct((b, l, n), out_dtype),
        grid=(b, l // tm),
        in_specs=[_row_spec(tm, d), _const_spec((1, d)), _mod_spec(d, jsc), _mod_spec(d, jsh),
                  _const_spec((d, n))],
        out_specs=_row_spec(tm, n),
        compiler_params=_params(("parallel", "parallel")),
        name="norm_proj",
    )(x, g.reshape(1, d), mod, mod, w)


def _norm_proj_tm_kernel(x_ref, g_ref, sc_ref, sh_ref, w_ref, *rest):
    _norm_proj_kernel(x_ref, g_ref, sc_ref, sh_ref, w_ref, rest[-1])


def _norm_proj_time_major(x, g, mod, jsc, jsh, w, tm, buf, boff, b_total):
    b, l, d = x.shape
    n = w.shape[1]
    in_specs = [_row_spec(tm, d), _const_spec((1, d)), _mod_spec(d, jsc), _mod_spec(d, jsh), _const_spec((d, n))]
    args = [x, g.reshape(1, d), mod, mod, w]
    aliases = {}
    if buf is not None:
        in_specs.append(pl.BlockSpec(memory_space=pl.ANY))
        args.append(buf)
        aliases = {len(args) - 1: 0}
    return pl.pallas_call(
        _norm_proj_tm_kernel,
        out_shape=jax.ShapeDtypeStruct((l, b_total * n), F32),
        grid=(b, l // tm),
        in_specs=in_specs,
        out_specs=pl.BlockSpec((tm, n), lambda bi, ti: (ti, boff + bi)),
        input_output_aliases=aliases,
        compiler_params=_params(("parallel", "parallel")),
        name="norm_proj_tm",
    )(*args)


def _proj_res_kernel(y_ref, w_ref, x_ref, gt_ref, o_ref):
    o_ref[...] = x_ref[...] + gt_ref[...] * jnp.dot(y_ref[...], w_ref[...], preferred_element_type=F32)


def _glu_res_kernel(y_ref, w_ref, x_ref, gt_ref, o_ref):
    d = o_ref.shape[-1]
    glu = jnp.dot(y_ref[...].astype(BF16), w_ref[...], preferred_element_type=F32)
    o_ref[...] = x_ref[...] + gt_ref[...] * (glu[:, :d] * jax.nn.sigmoid(glu[:, d:]))


def _proj_res(kernel, y, w, x, mod, jgt, tm, name, y_boff=None):
    b, l, d = x.shape
    k, n = w.shape
    y_spec = _row_spec(tm, k) if y_boff is None else pl.BlockSpec((tm, k), lambda bi, ti: (ti, y_boff + bi))
    return pl.pallas_call(
        kernel,
        out_shape=jax.ShapeDtypeStruct((b, l, d), F32),
        grid=(b, l // tm),
        in_specs=[y_spec, _const_spec((k, n)), _row_spec(tm, d), _mod_spec(d, jgt)],
        out_specs=_row_spec(tm, d),
        compiler_params=_params(("parallel", "parallel")),
        name=name,
    )(y, w, x, mod)


def _gla_proj_kernel(x_ref, g_ref, sc_ref, sh_ref, w_ref, w1_ref, w2_ref, b2_ref, p_ref, la_ref, *, dk_total, qscale):
    h = _normmod(x_ref[...], g_ref[...], sc_ref[...], sh_ref[...]).astype(BF16)
    proj = jnp.dot(h, w_ref[...], preferred_element_type=F32)
    p_ref[:, :dk_total] = (proj[:, :dk_total] * qscale).astype(BF16)
    p_ref[:, dk_total:] = proj[:, dk_total:].astype(BF16)
    t1 = jnp.dot(h, w1_ref[...], preferred_element_type=F32).astype(BF16)
    gk = jnp.dot(t1, w2_ref[...], preferred_element_type=F32) + b2_ref[...]
    la_ref[...] = _log_sigmoid(gk) * (1.0 / GLA_GATE_NORM)


def _gla_scan_kernel(p_ref, la_ref, gon_ref, o_ref, st_ref, *, nb, tc, dk, dv):
    heads = GLA_HEADS
    dk_total, dv_total = heads * dk, heads * dv

    @pl.when(pl.program_id(1) == 0)
    def _():
        st_ref[...] = jnp.zeros_like(st_ref)

    rr = lax.broadcasted_iota(jnp.int32, (CHUNK, CHUNK), 0)
    cc = lax.broadcasted_iota(jnp.int32, (CHUNK, CHUNK), 1)
    tri = (rr >= cc).astype(F32)

    def chunk(j, carry):
        rows = pl.ds(pl.multiple_of(j * CHUNK, CHUNK), CHUNK)
        for bi in range(nb):
            cum_all = jnp.dot(tri, la_ref[bi, rows, :], preferred_element_type=F32, precision=HIGHEST)
            for h in range(heads):
                ks = slice(h * dk, (h + 1) * dk)
                vs = slice(h * dv, (h + 1) * dv)
                cum = cum_all[:, ks]
                tot = cum[CHUNK - 1:CHUNK, :]
                kblk = p_ref[bi, rows, dk_total + h * dk:dk_total + (h + 1) * dk].astype(F32)
                kd = (kblk * jnp.exp(tot - cum)).astype(BF16)
                v = p_ref[bi, rows, 2 * dk_total + h * dv:2 * dk_total + (h + 1) * dv]
                upd = lax.dot_general(v, kd, (((0,), (0,)), ((), ())), preferred_element_type=F32)
                st = jnp.exp(tot) * st_ref[bi, h] + upd
                st_ref[bi, h] = st
                q = p_ref[bi, rows, ks]
                o = lax.dot_general(q, st.astype(BF16), (((1,), (1,)), ((), ())), preferred_element_type=F32)
                o = o * lax.rsqrt(jnp.mean(o * o, axis=-1, keepdims=True) + EPS) * gon_ref[...]
                g0 = 2 * dk_total + dv_total + h * dv
                g = p_ref[bi, rows, g0:g0 + dv].astype(F32)
                o_ref[bi, rows, vs] = (o * _silu(g)).astype(BF16)
        return carry

    lax.fori_loop(0, tc // CHUNK, chunk, 0)


def _gla_mixer(x, g_mix, mod, w_in, w_gk1, w_gk2, b_gk, g_onorm, w_out, tiles):
    b, l, d = x.shape
    tm, tc = tiles["row"], tiles["gla"]
    rank = w_gk1.shape[1]
    dk_total = w_gk2.shape[1]
    dv_total = w_out.shape[0]
    dk, dv = dk_total // GLA_HEADS, dv_total // GLA_HEADS
    n = w_in.shape[1]
    w1 = jnp.zeros((d, LANES), BF16).at[:, :rank].set(w_gk1.astype(BF16))
    w2 = jnp.zeros((LANES, dk_total), BF16).at[:rank, :].set(w_gk2.astype(BF16))
    p, la = pl.pallas_call(
        functools.partial(_gla_proj_kernel, dk_total=dk_total, qscale=dk ** -0.5),
        out_shape=(jax.ShapeDtypeStruct((b, l, n), BF16), jax.ShapeDtypeStruct((b, l, dk_total), F32)),
        grid=(b, l // tm),
        in_specs=[_row_spec(tm, d), _const_spec((1, d)), _mod_spec(d, 1), _mod_spec(d, 0),
                  _const_spec((d, n)), _const_spec((d, LANES)), _const_spec((LANES, dk_total)),
                  _const_spec((1, dk_total))],
        out_specs=(_row_spec(tm, n), _row_spec(tm, dk_total)),
        compiler_params=_params(("parallel", "parallel")),
        name="gla_proj",
    )(x, g_mix.reshape(1, d), mod, mod, w_in.astype(BF16), w1, w2, b_gk.reshape(1, dk_total))
    nb = tiles["gla_batch"] if b % tiles["gla_batch"] == 0 else 1

    def scan_spec(width):
        return pl.BlockSpec((nb, tc, width), lambda bi, ti: (bi, ti, 0))

    o = pl.pallas_call(
        functools.partial(_gla_scan_kernel, nb=nb, tc=tc, dk=dk, dv=dv),
        out_shape=jax.ShapeDtypeStruct((b, l, dv_total), BF16),
        grid=(b // nb, l // tc),
        in_specs=[scan_spec(n), scan_spec(dk_total), _const_spec((1, dv))],
        out_specs=scan_spec(dv_total),
        scratch_shapes=[pltpu.VMEM((nb, GLA_HEADS, dv, dk), F32)],
        compiler_params=_params(("arbitrary", "arbitrary")),
        name="gla_scan",
    )(p, la, g_onorm.reshape(1, dv))
    return _proj_res(_proj_res_kernel, o, w_out.astype(BF16), x, mod, 2, tm, "gla_out")


def _mlstm_kernel(xz_ref, cw_ref, cb_ref, wq_ref, wk_ref, wv_ref, wif_ref, bif_ref, skip_ref, gn_ref,
                  o_ref, ext_ref, q_ref, k_ref, v_ref, xc_ref, c_ref, n_ref, m_ref, *, tc, inner):
    heads = MLSTM_HEADS
    dh = inner // heads
    conv_w = cw_ref.shape[0]
    halo = SUBLANES

    @pl.when(pl.program_id(1) == 0)
    def _():
        ext_ref[0:halo, :] = jnp.zeros((halo, inner), F32)
        c_ref[...] = jnp.zeros_like(c_ref)
        n_ref[...] = jnp.zeros_like(n_ref)
        m_ref[...] = jnp.zeros_like(m_ref)

    ext_ref[halo:halo + tc, :] = xz_ref[:, :inner].astype(F32)
    acc = jnp.broadcast_to(cb_ref[...], (tc, inner))
    for j in range(conv_w):
        off = halo - (conv_w - 1) + j
        acc = acc + cw_ref[j:j + 1, :] * ext_ref[off:off + tc, :]
    ext_ref[0:halo, :] = ext_ref[tc:tc + halo, :]
    xc = _silu(acc)
    xc_ref[...] = xc

    gates = jnp.broadcast_to(bif_ref[...], (tc, LANES))
    for i in range(inner // MXU):
        sl = slice(i * MXU, (i + 1) * MXU)
        xcb = xc[:, sl].astype(BF16)
        q = jnp.dot(xcb, wq_ref[i], preferred_element_type=F32)
        k = jnp.dot(xcb, wk_ref[i], preferred_element_type=F32)
        v = jnp.dot(xz_ref[:, sl], wv_ref[i], preferred_element_type=F32)
        qb, kb, vb = q.astype(BF16), k.astype(BF16), v.astype(BF16)
        q_ref[:, sl] = qb
        k_ref[:, sl] = (k * dh ** -0.5).astype(BF16)
        v_ref[:, sl] = vb
        gates = gates + jnp.dot(qb, wif_ref[i * MXU:(i + 1) * MXU, :], preferred_element_type=F32)
        gates = gates + jnp.dot(kb, wif_ref[inner + i * MXU:inner + (i + 1) * MXU, :], preferred_element_type=F32)
        gates = gates + jnp.dot(vb, wif_ref[2 * inner + i * MXU:2 * inner + (i + 1) * MXU, :],
                                preferred_element_type=F32)
    rr = lax.broadcasted_iota(jnp.int32, (tc, tc), 0)
    cc = lax.broadcasted_iota(jnp.int32, (tc, tc), 1)
    cum = jnp.dot((rr >= cc).astype(F32), _log_sigmoid(gates), preferred_element_type=F32, precision=HIGHEST)
    a_all = gates - pltpu.roll(cum, LANES - heads, 1)
    a_rows = a_all.T
    chunk_r = lax.broadcasted_iota(jnp.int32, (tc, 1), 0) // CHUNK
    chunk_c = lax.broadcasted_iota(jnp.int32, (1, tc), 1) // CHUNK
    visible = chunk_c <= chunk_r
    n_chunks = tc // CHUNK
    for h in range(heads):
        hs = slice(h * dh, (h + 1) * dh)
        a_col = a_all[:, h:h + 1]
        f_col = cum[:, heads + h:heads + h + 1]
        m_in = m_ref[h][:, 0:1]
        mu = m_in
        mu_col = jnp.zeros((tc, 1), F32)
        f_end = jnp.zeros((tc, 1), F32)
        for c in range(n_chunks):
            mu = jnp.maximum(mu, jnp.max(jnp.where(chunk_r == c, a_col, NEG_BIG), axis=0, keepdims=True))
            mu_col = jnp.where(chunk_r == c, mu, mu_col)
            f_end = jnp.where(chunk_r == c, f_col[(c + 1) * CHUNK - 1:(c + 1) * CHUNK, :], f_end)
        q = q_ref[:, hs]
        k = k_ref[:, hs]
        v = v_ref[:, hs]
        w = jnp.where(visible, jnp.exp(a_rows[h:h + 1, :] - mu_col), 0.0)
        p = lax.dot_general(q, k, (((1,), (1,)), ((), ())), preferred_element_type=F32) * w
        carry_in = jnp.exp(m_in - mu_col)
        num = jnp.dot(p.astype(BF16), v, preferred_element_type=F32) \
            + carry_in * jnp.dot(q, c_ref[h].astype(BF16), preferred_element_type=F32)
        qn = carry_in * jnp.sum(q.astype(F32) * n_ref[h], axis=-1, keepdims=True) + jnp.sum(p, axis=-1, keepdims=True)
        den = jnp.maximum(jnp.abs(qn), jnp.exp(-(mu_col + f_end)))
        hc = num / den
        mean = jnp.mean(hc, axis=-1, keepdims=True)
        var = jnp.mean((hc - mean) * (hc - mean), axis=-1, keepdims=True)
        hn = (hc - mean) * lax.rsqrt(var + EPS) * gn_ref[:, hs]
        z = xz_ref[:, inner + h * dh:inner + (h + 1) * dh].astype(F32)
        o_ref[:, hs] = ((hn + skip_ref[:, hs] * xc_ref[:, hs]) * _silu(z)).astype(BF16)
        decay = jnp.exp(m_in - mu)
        kw = k.astype(F32) * jnp.exp(a_col - mu)
        c_ref[h] = decay * c_ref[h] + lax.dot_general(kw.astype(BF16), v, (((0,), (0,)), ((), ())),
                                                      preferred_element_type=F32)
        n_ref[h] = decay * n_ref[h] + jnp.sum(kw, axis=0, keepdims=True)
        m_ref[h] = jnp.broadcast_to(mu + f_col[tc - 1:tc, :], (1, LANES))


def _diag_tiles(w):
    nblk, bs, _ = w.shape
    per = MXU // bs
    wt = w.reshape(nblk // per, per, bs, bs)
    eye = jnp.eye(per, dtype=w.dtype)
    return jnp.einsum("tpio,pq->tpiqo", wt, eye).reshape(nblk // per, MXU, MXU).astype(BF16)


def _mlstm_mixer(x, g_mix, mod, w_up, conv_w, conv_b, w_q, w_k, w_v, w_if, b_if, skip, g_norm, w_down, tiles):
    b, l, d = x.shape
    tm, tc = tiles["row"], tiles["mlstm"]
    inner = w_down.shape[0]
    heads = MLSTM_HEADS
    xz = _norm_proj(x, g_mix, mod, 1, 0, w_up.astype(BF16), BF16, tm)
    wif = jnp.zeros((3 * inner, LANES), BF16).at[:, :2 * heads].set(w_if.astype(BF16))
    bif = jnp.zeros((1, LANES), F32).at[0, :2 * heads].set(b_if)
    nt = inner // MXU
    y = pl.pallas_call(
        functools.partial(_mlstm_kernel, tc=tc, inner=inner),
        out_shape=jax.ShapeDtypeStruct((b, l, inner), BF16),
        grid=(b, l // tc),
        in_specs=[_row_spec(tc, 2 * inner), _const_spec(conv_w.shape), _const_spec((1, inner)),
                  _const_spec((nt, MXU, MXU)), _const_spec((nt, MXU, MXU)), _const_spec((nt, MXU, MXU)),
                  _const_spec((3 * inner, LANES)), _const_spec((1, LANES)), _const_spec((1, inner)),
                  _const_spec((1, inner))],
        out_specs=_row_spec(tc, inner),
        scratch_shapes=[pltpu.VMEM((tc + 2 * SUBLANES, inner), F32),
                        pltpu.VMEM((tc, inner), BF16), pltpu.VMEM((tc, inner), BF16),
                        pltpu.VMEM((tc, inner), BF16), pltpu.VMEM((tc, inner), F32),
                        pltpu.VMEM((heads, inner // heads, inner // heads), F32),
                        pltpu.VMEM((heads, 1, inner // heads), F32),
                        pltpu.VMEM((heads, 1, LANES), F32)],
        compiler_params=_params(("arbitrary", "arbitrary")),
        name="mlstm_scan",
    )(xz, conv_w, conv_b.reshape(1, inner), _diag_tiles(w_q), _diag_tiles(w_k), _diag_tiles(w_v), wif, bif,
      skip.reshape(1, inner), g_norm.reshape(1, inner))
    return _proj_res(_proj_res_kernel, y, w_down.astype(BF16), x, mod, 2, tm, "mlstm_out")


def _s5_kernel(u_ref, bt_ref, ct_ref, are_ref, aim_ref, d_ref, y_ref, bu_ref, st_ref, *, ts, nb, d, npart, per):
    rows = ts * nb
    half = per // 2
    ntile = npart // MXU

    @pl.when(pl.program_id(0) == 0)
    def _():
        st_ref[...] = jnp.zeros_like(st_ref)

    def col_tile(t, j):
        return half * t + j if j < half else ntile + half * t + (j - half)

    u = u_ref[...].reshape(rows, d)
    ub = u.astype(BF16)
    for kt in range(d // MXU):
        lhs = ub[:, kt * MXU:(kt + 1) * MXU]
        for j in range(per):
            n = col_tile(kt, j)
            bu_ref[:, n * MXU:(n + 1) * MXU] = jnp.dot(lhs, bt_ref[kt * per + j], preferred_element_type=F32)

    cb = min(512, npart)
    for c in range(npart // cb):
        re = slice(c * cb, (c + 1) * cb)
        im = slice(npart + c * cb, npart + (c + 1) * cb)
        ar = are_ref[:, re]
        ai = aim_ref[:, re]

        def step(t, carry):
            xr, xi = carry
            r = pl.ds(pl.multiple_of(t * nb, nb), nb)
            nr = ar * xr - ai * xi + bu_ref[r, re]
            ni = ar * xi + ai * xr + bu_ref[r, im]
            bu_ref[r, re] = nr
            bu_ref[r, im] = ni
            return nr, ni

        xr, xi = lax.fori_loop(0, ts, step, (st_ref[:, re], st_ref[:, im]), unroll=4)
        st_ref[:, re] = xr
        st_ref[:, im] = xi

    for nt in range(d // MXU):
        acc = jnp.zeros((rows, MXU), F32)
        for j in range(per):
            n = col_tile(nt, j)
            acc = acc + jnp.dot(bu_ref[:, n * MXU:(n + 1) * MXU].astype(BF16), ct_ref[nt * per + j],
                                preferred_element_type=F32)
        ch = slice(nt * MXU, (nt + 1) * MXU)
        yv = acc + d_ref[:, ch] * u[:, ch]
        y_ref[:, :, ch] = jax.nn.gelu(yv).reshape(ts, nb, MXU)


def _s5_mixer(xs, g_mix, mods, w_in, a_re, a_im, log_dt, b_re, b_im, c_re, c_im, d_skip, w_out, tiles):
    l, d = xs[0].shape[1:]
    b = sum(p.shape[0] for p in xs)
    tm, ts = tiles["row"], tiles["s5"]
    groups, pst = a_re.shape
    npart = groups * pst
    gk = MXU // S5_GROUP
    gn = MXU // pst
    per = 2 * (gk // gn)
    nkt = d // MXU

    a = lax.complex(a_re, a_im)
    dt = jnp.exp(log_dt)[:, None]
    a_bar = jnp.exp(a * dt)
    b_bar = ((a_bar - 1.0) / a)[..., None] * lax.complex(b_re, b_im)
    eye = jnp.eye(groups, dtype=F32)
    bfull = jnp.concatenate([jnp.einsum("gpc,gh->gchp", part, eye).reshape(d, npart)
                             for part in (jnp.real(b_bar), jnp.imag(b_bar))], axis=1)
    cfull = jnp.concatenate([jnp.einsum("gcp,gh->gphc", part, eye).reshape(npart, d)
                             for part in (c_re, -c_im)], axis=0)
    half = per // 2
    ntile = npart // MXU

    def col_tile(t, j):
        return half * t + j if j < half else ntile + half * t + (j - half)

    bt = jnp.stack([bfull[kt * MXU:(kt + 1) * MXU, col_tile(kt, j) * MXU:(col_tile(kt, j) + 1) * MXU]
                    for kt in range(nkt) for j in range(per)]).astype(BF16)
    ct = jnp.stack([cfull[col_tile(nt, j) * MXU:(col_tile(nt, j) + 1) * MXU, nt * MXU:(nt + 1) * MXU]
                    for nt in range(nkt) for j in range(per)]).astype(BF16)
    are = jnp.broadcast_to(jnp.real(a_bar).reshape(1, npart), (b, npart))
    aim = jnp.broadcast_to(jnp.imag(a_bar).reshape(1, npart), (b, npart))

    win = w_in.astype(BF16)
    ut, b0 = None, 0
    for x, mod in zip(xs, mods):
        ut = _norm_proj_time_major(x, g_mix, mod, 1, 0, win, tm, ut, b0, b)
        b0 += x.shape[0]
    ut = ut.reshape(l, b, d)
    yt = pl.pallas_call(
        functools.partial(_s5_kernel, ts=ts, nb=b, d=d, npart=npart, per=per),
        out_shape=jax.ShapeDtypeStruct((l, b, d), F32),
        grid=(l // ts,),
        in_specs=[pl.BlockSpec((ts, b, d), lambda t: (t, 0, 0)),
                  pl.BlockSpec((nkt * per, MXU, MXU), lambda t: (0, 0, 0)),
                  pl.BlockSpec((nkt * per, MXU, MXU), lambda t: (0, 0, 0)),
                  pl.BlockSpec((b, npart), lambda t: (0, 0)),
                  pl.BlockSpec((b, npart), lambda t: (0, 0)),
                  pl.BlockSpec((1, d), lambda t: (0, 0))],
        out_specs=pl.BlockSpec((ts, b, d), lambda t: (t, 0, 0)),
        scratch_shapes=[pltpu.VMEM((ts * b, 2 * npart), F32), pltpu.VMEM((b, 2 * npart), F32)],
        compiler_params=_params(("arbitrary",)),
        name="s5_scan",
    )(ut, bt, ct, are, aim, d_skip.reshape(1, d))
    yt = yt.reshape(l, b * d)
    wout = w_out.astype(BF16)
    outs, b0 = [], 0
    for x, mod in zip(xs, mods):
        outs.append(_proj_res(_glu_res_kernel, yt, wout, x, mod, 2, tm, "s5_out", y_boff=b0))
        b0 += x.shape[0]
    return outs


def _router_kernel(x_ref, g_ref, sc_ref, sh_ref, wr_ref, br_ref, h_ref, ri_ref, rw_ref, cnt_ref, run_ref, *, tm):
    @pl.when((pl.program_id(0) == 0) & (pl.program_id(1) == 0))
    def _():
        run_ref[...] = jnp.zeros_like(run_ref)

    hb = _normmod(x_ref[...], g_ref[...], sc_ref[...], sh_ref[...]).astype(BF16)
    h_ref[...] = hb
    logits = jnp.dot(hb, wr_ref[...], preferred_element_type=F32) + br_ref[...]
    lane = lax.broadcasted_iota(jnp.int32, logits.shape, 1)
    lane_f = lane.astype(F32)
    work = logits
    vals, idxs = [], []
    for _ in range(TOP_K):
        m = jnp.max(work, axis=-1, keepdims=True)
        idx = jnp.min(jnp.where(work == m, lane_f, float(LANES)), axis=-1, keepdims=True)
        vals.append(m)
        idxs.append(idx)
        work = jnp.where(lane_f == idx, 2.0 * NEG_BIG, work)
    exps = [jnp.exp(v - vals[0]) for v in vals]
    denom = exps[0]
    for e in exps[1:]:
        denom = denom + e
    multi = jnp.zeros(logits.shape, F32)
    for idx in idxs:
        multi = multi + (lane_f == idx).astype(F32)
    rr = lax.broadcasted_iota(jnp.int32, (tm, tm), 0)
    cc = lax.broadcasted_iota(jnp.int32, (tm, tm), 1)
    before = (rr > cc).astype(BF16)
    cs = jnp.dot(before, multi.astype(BF16), preferred_element_type=F32) + run_ref[0:1, :]
    run = run_ref[0:1, :] + jnp.sum(multi, axis=0, keepdims=True)
    run_ref[...] = jnp.broadcast_to(run, run_ref.shape)
    cnt_ref[...] = jnp.broadcast_to(run, cnt_ref.shape).astype(jnp.int32)
    ri = jnp.zeros(logits.shape, F32)
    rw = jnp.zeros(logits.shape, F32)
    for k in range(TOP_K):
        rank = jnp.sum(jnp.where(lane_f == idxs[k], cs, 0.0), axis=-1, keepdims=True)
        ri = ri + jnp.where(lane == k, idxs[k], 0.0) + jnp.where(lane == TOP_K + k, rank, 0.0)
        rw = rw + jnp.where(lane == k, exps[k] / denom, 0.0)
    ri_ref[...] = ri.T[:2 * TOP_K, :].astype(jnp.int32)
    rw_ref[...] = rw


def _dest_kernel(offs_ref, ri_ref, o_ref, *, n_exp):
    idx = ri_ref[0:TOP_K, :]
    off = jnp.zeros(idx.shape, jnp.int32)
    for e in range(n_exp):
        off = jnp.where(idx == e, offs_ref[e], off)
    o_ref[0:TOP_K, :] = off + ri_ref[TOP_K:2 * TOP_K, :]
    o_ref[TOP_K:2 * TOP_K, :] = idx


def _expert_kernel(te_ref, nv_ref, xs_ref, wgu_ref, bgu_ref, wd_ref, bd_ref, o_ref, *rest, dff, cast):
    i = pl.program_id(0)
    valid = i < nv_ref[0]
    if cast:
        wgu_bf, wd_bf, act_ref = rest
        cast_rows = 2 * SUBLANES * SUBLANES

        @pl.when(valid & ((i == 0) | (te_ref[i] != te_ref[jnp.maximum(i - 1, 0)])))
        def _():
            def cast_gu(r, carry):
                rows = pl.ds(pl.multiple_of(r * cast_rows, cast_rows), cast_rows)
                wgu_bf[rows, :] = wgu_ref[rows, :].astype(BF16)
                return carry

            def cast_d(r, carry):
                rows = pl.ds(pl.multiple_of(r * cast_rows, cast_rows), cast_rows)
                wd_bf[rows, :] = wd_ref[rows, :].astype(BF16)
                return carry

            lax.fori_loop(0, wgu_ref.shape[0] // cast_rows, cast_gu, 0)
            lax.fori_loop(0, wd_ref.shape[0] // cast_rows, cast_d, 0)
    else:
        wgu_bf, wd_bf = wgu_ref, wd_ref
        (act_ref,) = rest

    @pl.when(valid)
    def _():
        x = xs_ref[...]
        gate = jnp.dot(x, wgu_bf[:, :dff], preferred_element_type=F32) + bgu_ref[:, :dff]
        up = jnp.dot(x, wgu_bf[:, dff:], preferred_element_type=F32) + bgu_ref[:, dff:]
        gate = jnp.minimum(gate, SWIGLU_LIMIT)
        up = jnp.clip(up, -SWIGLU_LIMIT, SWIGLU_LIMIT)
        act_ref[...] = ((up + 1.0) * gate * jax.nn.sigmoid(SWIGLU_ALPHA * gate)).astype(BF16)
        o_ref[...] = (jnp.dot(act_ref[...], wd_bf[...], preferred_element_type=F32) + bd_ref[...]).astype(o_ref.dtype)

    @pl.when(jnp.logical_not(valid))
    def _():
        o_ref[...] = jnp.zeros_like(o_ref)


def _combine_kernel(y0_ref, y1_ref, y2_ref, y3_ref, rw_ref, x_ref, gt_ref, gf_ref, *rest, final):
    o_ref = rest[-1]
    acc = jnp.zeros(o_ref.shape, F32)
    for k, y_ref in enumerate((y0_ref, y1_ref, y2_ref, y3_ref)):
        acc = acc + rw_ref[:, k:k + 1] * y_ref[...].astype(F32)
    out = x_ref[...] + gt_ref[...] * acc
    if final:
        out = out * lax.rsqrt(jnp.mean(out * out, axis=-1, keepdims=True) + EPS) * gf_ref[...]
    o_ref[...] = out


def _moe_route(x, g_ffn, mod, w_router, b_router, tiles):
    b, l, d = x.shape
    t = b * l
    tm, te = tiles["row"], tiles["expert"]
    n_exp = w_router.shape[1]
    assert TOP_K == 4
    wr = jnp.zeros((d, LANES), BF16).at[:, :n_exp].set(w_router.astype(BF16))
    br = jnp.full((1, LANES), NEG_BIG, F32).at[0, :n_exp].set(b_router)
    nt = l // tm
    h2, ri, rw, cnt = pl.pallas_call(
        functools.partial(_router_kernel, tm=tm),
        out_shape=(jax.ShapeDtypeStruct((b, l, d), BF16), jax.ShapeDtypeStruct((2 * TOP_K, t), jnp.int32),
                   jax.ShapeDtypeStruct((b, l, LANES), F32), jax.ShapeDtypeStruct((SUBLANES, LANES), jnp.int32)),
        grid=(b, nt),
        in_specs=[_row_spec(tm, d), _const_spec((1, d)), _mod_spec(d, 4), _mod_spec(d, 3),
                  _const_spec((d, LANES)), _const_spec((1, LANES))],
        out_specs=(_row_spec(tm, d), pl.BlockSpec((2 * TOP_K, tm), lambda bi, ti: (0, bi * nt + ti)),
                   _row_spec(tm, LANES), _const_spec((SUBLANES, LANES))),
        scratch_shapes=[pltpu.VMEM((SUBLANES, LANES), F32)],
        compiler_params=_params(("arbitrary", "arbitrary")),
        name="moe_router",
    )(x, g_ffn.reshape(1, d), mod, mod, wr, br)

    counts = cnt[0, :n_exp]
    padded = jnp.maximum((counts + te - 1) // te, 1) * te
    ends = jnp.cumsum(padded)
    offs = ends - padded
    n_rows = t * TOP_K + n_exp * te
    n_tiles = n_rows // te
    tcol = min(t, 8192)
    dest = pl.pallas_call(
        functools.partial(_dest_kernel, n_exp=n_exp),
        out_shape=jax.ShapeDtypeStruct((2 * TOP_K, t), jnp.int32),
        grid_spec=pltpu.PrefetchScalarGridSpec(
            num_scalar_prefetch=1,
            grid=(t // tcol,),
            in_specs=[pl.BlockSpec((2 * TOP_K, tcol), lambda i, o: (0, i))],
            out_specs=pl.BlockSpec((2 * TOP_K, tcol), lambda i, o: (0, i))),
        compiler_params=_params(("parallel",)),
        name="moe_dest",
    )(offs.astype(jnp.int32), ri)[:TOP_K]
    slot = jnp.arange(te, dtype=jnp.int32)[None, :]
    n_pad = (padded - counts)[:, None]
    n_tail = te - n_pad
    tail_base = ends[-1] + (jnp.cumsum(n_tail[:, 0]) - n_tail[:, 0])[:, None]
    free_rows = jnp.where(slot < n_pad, (offs + counts)[:, None] + slot, tail_base + slot - n_pad)
    filler = (jnp.arange(n_exp * te, dtype=jnp.int32) * 61) % t
    keys = jnp.concatenate([dest.reshape(-1), free_rows.reshape(-1).astype(jnp.int32)])
    vals = jnp.concatenate([jnp.tile(jnp.arange(t, dtype=jnp.int32), TOP_K), filler])
    src = lax.sort((keys, vals), num_keys=1)[1]
    tile_start = jnp.arange(n_tiles, dtype=jnp.int32) * te
    tile_exp = jnp.minimum(jnp.sum(tile_start[:, None] >= ends[None, :], axis=1), n_exp - 1).astype(jnp.int32)
    n_valid = (ends[-1:] // te).astype(jnp.int32)
    return dict(h2=h2, src=src, dest=dest, rw=rw, tile_exp=tile_exp, n_valid=n_valid)


def _moe_dispatch(route):
    h2 = route["h2"]
    return h2.reshape(-1, h2.shape[-1]).at[route["src"]].get(mode="promise_in_bounds")


def _moe_experts(route, xs, w_gu, b_gu, w_down, b_down, tiles, layer=None):
    n_rows, d = xs.shape
    te = tiles["expert"]
    cast = layer is not None
    n_exp, dff = w_down.shape[-3:-1]
    n_tiles = n_rows // te
    if cast:
        def wspec(r, c):
            return pl.BlockSpec((None, None, r, c), lambda i, e, nv: (layer, e[i], 0, 0))
        b_gu, b_down = b_gu[layer], b_down[layer]
    else:
        def wspec(r, c):
            return pl.BlockSpec((None, r, c), lambda i, e, nv: (e[i], 0, 0))

    def espec(r, c):
        return pl.BlockSpec((None, r, c), lambda i, e, nv: (e[i], 0, 0))

    rows_spec = pl.BlockSpec((te, d), lambda i, e, nv: (i, 0))
    out_shape = [jax.ShapeDtypeStruct((n_rows, d), BF16)]
    out_specs = [rows_spec]
    if cast:
        out_shape += [jax.ShapeDtypeStruct((n_exp, d, 2 * dff), BF16), jax.ShapeDtypeStruct((n_exp, dff, d), BF16)]
        out_specs += [espec(d, 2 * dff), espec(dff, d)]
    res = pl.pallas_call(
        functools.partial(_expert_kernel, dff=dff, cast=cast),
        out_shape=tuple(out_shape),
        grid_spec=pltpu.PrefetchScalarGridSpec(
            num_scalar_prefetch=2,
            grid=(n_tiles,),
            in_specs=[rows_spec, wspec(d, 2 * dff), espec(1, 2 * dff), wspec(dff, d), espec(1, d)],
            out_specs=tuple(out_specs),
            scratch_shapes=[pltpu.VMEM((te, dff), BF16)]),
        compiler_params=_params(("arbitrary",)),
        name="moe_experts_cast" if cast else "moe_experts",
    )(route["tile_exp"], route["n_valid"], xs, w_gu, b_gu.reshape(n_exp, 1, 2 * dff), w_down,
      b_down.reshape(n_exp, 1, d))
    return res if cast else res[0]


def _moe_gather_back(route, ys):
    return [ys.at[route["dest"][k]].get(mode="promise_in_bounds") for k in range(TOP_K)]


def _moe_combine(x, yk, route, mod, g_final, final, tiles, out=None):
    b, l, d = x.shape
    t = b * l
    tm = tiles["row"]
    yk = [y.reshape(b, l, d) for y in yk]
    rw = route["rw"]
    in_specs = [_row_spec(tm, d)] * TOP_K + [_row_spec(tm, LANES), _row_spec(tm, d), _mod_spec(d, 5),
                                             _const_spec((1, d))]
    args = [*yk, rw, x, mod, g_final.reshape(1, d)]
    buf, boff, b_total = (None, 0, b) if out is None else out
    aliases = {}
    if buf is not None:
        in_specs.append(pl.BlockSpec(memory_space=pl.ANY))
        args.append(buf)
        aliases = {len(args) - 1: 0}
    return pl.pallas_call(
        functools.partial(_combine_kernel, final=final),
        out_shape=jax.ShapeDtypeStruct((b_total, l, d), F32),
        grid=(b, l // tm),
        in_specs=in_specs,
        out_specs=pl.BlockSpec((None, tm, d), lambda bi, ti: (boff + bi, ti, 0)),
        input_output_aliases=aliases,
        compiler_params=_params(("parallel", "parallel")),
        name="moe_combine",
    )(*args)


def kernel(x, c, g_mix, g_ffn, w_ada, b_ada, gla_w_in, gla_w_gk1, gla_w_gk2, gla_b_gk, gla_g_onorm, gla_w_out, ml_w_up, ml_conv_w, ml_conv_b, ml_w_q, ml_w_k, ml_w_v, ml_w_if, ml_b_if, ml_skip, ml_g_norm, ml_w_down, s5_w_in, s5_a_re, s5_a_im, s5_log_dt, s5_b_re, s5_b_im, s5_c_re, s5_c_im, s5_d, s5_w_out, moe_w_router, moe_b_router, moe_w_gu, moe_b_gu, moe_w_down, moe_b_down, g_final):
    b, l, d = x.shape
    depth = w_ada.shape[0]
    n_exp = moe_w_router.shape[-1]
    assert l % CHUNK == 0 and d % MXU == 0
    n_parts = 2 if b % 2 == 0 else 1
    bp = b // n_parts
    tiles = _tiles(l, bp * l, n_exp)
    mods = _modulation(c, w_ada, b_ada)
    xs = [x[s * bp:(s + 1) * bp] for s in range(n_parts)]
    after = lax.optimization_barrier
    for i in range(depth):
        mp = [mods[i, s * bp:(s + 1) * bp].reshape(bp, 6, 1, d) for s in range(n_parts)]
        kind, j = i % N_MIXERS, i // N_MIXERS
        last = i == depth - 1

        def mixer(xp, mod):
            if kind == 0:
                return _gla_mixer(xp, g_mix[i], mod, gla_w_in[j], gla_w_gk1[j], gla_w_gk2[j], gla_b_gk[j],
                                  gla_g_onorm[j], gla_w_out[j], tiles)
            return _mlstm_mixer(xp, g_mix[i], mod, ml_w_up[j], ml_conv_w[j], ml_conv_b[j], ml_w_q[j], ml_w_k[j],
                                ml_w_v[j], ml_w_if[j], ml_b_if[j], ml_skip[j], ml_g_norm[j], ml_w_down[j], tiles)

        def route(xp, mod):
            return _moe_route(xp, g_ffn[i], mod, moe_w_router[i], moe_b_router[i], tiles)

        def experts_casting(r, rows):
            return _moe_experts(r, rows, moe_w_gu, moe_b_gu, moe_w_down, moe_b_down, tiles, layer=i)

        joint = kind == 2
        if joint:
            xs = _s5_mixer(xs, g_mix[i], mp, s5_w_in[j], s5_a_re[j], s5_a_im[j], s5_log_dt[j], s5_b_re[j],
                           s5_b_im[j], s5_c_re[j], s5_c_im[j], s5_d[j], s5_w_out[j], tiles)
        if n_parts == 1:
            x0 = xs[0] if joint else mixer(xs[0], mp[0])
            r0 = route(x0, mp[0])
            y0 = _moe_gather_back(r0, experts_casting(r0, _moe_dispatch(r0))[0])
            xs = [_moe_combine(x0, y0, r0, mp[0], g_final, last, tiles)]
            continue
        xa, xb = xs
        xa = xa if joint else mixer(xa, mp[0])
        ra = route(xa, mp[0])
        ra["src"], xb = after((ra["src"], xb))
        rows_a = _moe_dispatch(ra)
        xb = xb if joint else mixer(xb, mp[1])
        rb = route(xb, mp[1])
        rows_a, rb["src"] = after((rows_a, rb["src"]))
        rows_b = _moe_dispatch(rb)
        ys_a, wgu, wdn = experts_casting(ra, rows_a)
        back_a = _moe_gather_back(ra, ys_a)
        ys_b = _moe_experts(rb, rows_b, wgu, moe_b_gu[i], wdn, moe_b_down[i], tiles)
        back_a, ys_b = after((back_a, ys_b))
        back_b = _moe_gather_back(rb, ys_b)
        xa = _moe_combine(xa, back_a, ra, mp[0], g_final, last, tiles, (None, 0, b) if last else None)
        xb = _moe_combine(xb, back_b, rb, mp[1], g_final, last, tiles, (xa, bp, b) if last else None)
        xs = [xa, xb]
    return xs[-1] if n_parts == 2 else xs[0]
```

```python
import functools

import jax
import jax.numpy as jnp
from jax import lax
from jax.experimental import pallas as pl
from jax.experimental.pallas import tpu as pltpu

F32 = jnp.float32
BF16 = jnp.bfloat16
HIGHEST = lax.Precision.HIGHEST

EPS = 1e-6
CHUNK = 64
N_MIXERS = 3
GLA_HEADS = 4
GLA_GATE_NORM = 16.0
MLSTM_HEADS = 4
S5_GROUP = 16
TOP_K = 4
SWIGLU_LIMIT = 7.0
SWIGLU_ALPHA = 1.702

LANES = 128
SUBLANES = 8
MXU = 256
VMEM_LIMIT_BYTES = 56 * 1024 * 1024

NEG_BIG = -1e30


def _tiles(seq, tokens, n_experts):
    row = min(1024, seq)
    gla = min(256, seq)
    gla_batch = 4
    mlstm = min(256, seq)
    s5 = min(64, seq)
    per_expert = tokens * TOP_K // n_experts
    expert = 512 if per_expert >= 2048 else 128
    return dict(row=row, gla=gla, gla_batch=gla_batch, mlstm=mlstm, s5=s5, expert=expert)


def _params(sem):
    return pltpu.CompilerParams(dimension_semantics=sem, vmem_limit_bytes=VMEM_LIMIT_BYTES)


def _normmod(x, g, sc, sh):
    y = x * lax.rsqrt(jnp.mean(x * x, axis=-1, keepdims=True) + EPS)
    return (y * g) * (1.0 + sc) + sh


def _log_sigmoid(x):
    return jnp.minimum(x, 0.0) - jnp.log(1.0 + jnp.exp(-jnp.abs(x)))


def _silu(x):
    return x * jax.nn.sigmoid(x)


def _mod_spec(d, j):
    return pl.BlockSpec((None, None, 1, d), lambda b, t: (b, j, 0, 0))


def _row_spec(tm, n):
    return pl.BlockSpec((None, tm, n), lambda b, t: (b, t, 0))


def _const_spec(shape):
    nd = len(shape)
    return pl.BlockSpec(shape, lambda b, t: (0,) * nd)


def _mod_kernel(c_ref, w_ref, b_ref, o_ref):
    ca = _silu(c_ref[...]).astype(BF16)
    o_ref[0] = jnp.dot(ca, w_ref[0].astype(BF16), preferred_element_type=F32) + b_ref[0]


def _modulation(c, w_ada, b_ada):
    depth, d, n = w_ada.shape
    b = c.shape[0]
    tn = n // 4
    return pl.pallas_call(
        _mod_kernel,
        out_shape=jax.ShapeDtypeStruct((depth, b, n), F32),
        grid=(depth, n // tn),
        in_specs=[pl.BlockSpec((b, d), lambda i, j: (0, 0)),
                  pl.BlockSpec((1, d, tn), lambda i, j: (i, 0, j)),
                  pl.BlockSpec((1, 1, tn), lambda i, j: (i, 0, j))],
        out_specs=pl.BlockSpec((1, b, tn), lambda i, j: (i, 0, j)),
        compiler_params=_params(("parallel", "parallel")),
        name="adaln_mod",
    )(c, w_ada, b_ada.reshape(depth, 1, n))


def _norm_proj_kernel(x_ref, g_ref, sc_ref, sh_ref, w_ref, o_ref):
    h = _normmod(x_ref[...], g_ref[...], sc_ref[...], sh_ref[...]).astype(BF16)
    o_ref[...] = jnp.dot(h, w_ref[...], preferred_element_type=F32).astype(o_ref.dtype)


def _norm_proj(x, g, mod, jsc, jsh, w, out_dtype, tm):
    b, l, d = x.shape
    n = w.shape[1]
    return pl.pallas_call(
        _norm_proj_kernel,
        out_shape=jax.ShapeDtypeStruct((b, l, n), out_dtype),
        grid=(b, l // tm),
        in_specs=[_row_spec(tm, d), _const_spec((1, d)), _mod_spec(d, jsc), _mod_spec(d, jsh),
                  _const_spec((d, n))],
        out_specs=_row_spec(tm, n),
        compiler_params=_params(("parallel", "parallel")),
        name="norm_proj",
    )(x, g.reshape(1, d), mod, mod, w)


def _norm_proj_tm_kernel(x_ref, g_ref, sc_ref, sh_ref, w_ref, *rest):
    _norm_proj_kernel(x_ref, g_ref, sc_ref, sh_ref, w_ref, rest[-1])


def _norm_proj_time_major(x, g, mod, jsc, jsh, w, tm, buf, boff, b_total):
    b, l, d = x.shape
    n = w.shape[1]
    in_specs = [_row_spec(tm, d), _const_spec((1, d)), _mod_spec(d, jsc), _mod_spec(d, jsh), _const_spec((d, n))]
    args = [x, g.reshape(1, d), mod, mod, w]
    aliases = {}
    if buf is not None:
        in_specs.append(pl.BlockSpec(memory_space=pl.ANY))
        args.append(buf)
        aliases = {len(args) - 1: 0}
    return pl.pallas_call(
        _norm_proj_tm_kernel,
        out_shape=jax.ShapeDtypeStruct((l, b_total * n), F32),
        grid=(b, l // tm),
        in_specs=in_specs,
        out_specs=pl.BlockSpec((tm, n), lambda bi, ti: (ti, boff + bi)),
        input_output_aliases=aliases,
        compiler_params=_params(("parallel", "parallel")),
        name="norm_proj_tm",
    )(*args)


def _proj_res_kernel(y_ref, w_ref, x_ref, gt_ref, o_ref):
    o_ref[...] = x_ref[...] + gt_ref[...] * jnp.dot(y_ref[...], w_ref[...], preferred_element_type=F32)


def _glu_res_kernel(y_ref, w_ref, x_ref, gt_ref, o_ref):
    d = o_ref.shape[-1]
    glu = jnp.dot(y_ref[...].astype(BF16), w_ref[...], preferred_element_type=F32)
    o_ref[...] = x_ref[...] + gt_ref[...] * (glu[:, :d] * jax.nn.sigmoid(glu[:, d:]))


def _proj_res(kernel, y, w, x, mod, jgt, tm, name, y_boff=None):
    b, l, d = x.shape
    k, n = w.shape
    y_spec = _row_spec(tm, k) if y_boff is None else pl.BlockSpec((tm, k), lambda bi, ti: (ti, y_boff + bi))
    return pl.pallas_call(
        kernel,
        out_shape=jax.ShapeDtypeStruct((b, l, d), F32),
        grid=(b, l // tm),
        in_specs=[y_spec, _const_spec((k, n)), _row_spec(tm, d), _mod_spec(d, jgt)],
        out_specs=_row_spec(tm, d),
        compiler_params=_params(("parallel", "parallel")),
        name=name,
    )(y, w, x, mod)


def _gla_proj_kernel(x_ref, g_ref, sc_ref, sh_ref, w_ref, w1_ref, w2_ref, b2_ref, p_ref, la_ref, *, dk_total, qscale):
    h = _normmod(x_ref[...], g_ref[...], sc_ref[...], sh_ref[...]).astype(BF16)
    proj = jnp.dot(h, w_ref[...], preferred_element_type=F32)
    p_ref[:, :dk_total] = (proj[:, :dk_total] * qscale).astype(BF16)
    p_ref[:, dk_total:] = proj[:, dk_total:].astype(BF16)
    t1 = jnp.dot(h, w1_ref[...], preferred_element_type=F32).astype(BF16)
    gk = jnp.dot(t1, w2_ref[...], preferred_element_type=F32) + b2_ref[...]
    la_ref[...] = _log_sigmoid(gk) * (1.0 / GLA_GATE_NORM)


def _gla_scan_kernel(p_ref, la_ref, gon_ref, o_ref, st_ref, *, nb, tc, dk, dv):
    heads = GLA_HEADS
    dk_total, dv_total = heads * dk, heads * dv

    @pl.when(pl.program_id(1) == 0)
    def _():
        st_ref[...] = jnp.zeros_like(st_ref)

    rr = lax.broadcasted_iota(jnp.int32, (CHUNK, CHUNK), 0)
    cc = lax.broadcasted_iota(jnp.int32, (CHUNK, CHUNK), 1)
    tri = (rr >= cc).astype(F32)

    def chunk(j, carry):
        rows = pl.ds(pl.multiple_of(j * CHUNK, CHUNK), CHUNK)
        for bi in range(nb):
            cum_all = jnp.dot(tri, la_ref[bi, rows, :], preferred_element_type=F32, precision=HIGHEST)
            for h in range(heads):
                ks = slice(h * dk, (h + 1) * dk)
                vs = slice(h * dv, (h + 1) * dv)
                cum = cum_all[:, ks]
                tot = cum[CHUNK - 1:CHUNK, :]
                kblk = p_ref[bi, rows, dk_total + h * dk:dk_total + (h + 1) * dk].astype(F32)
                kd = (kblk * jnp.exp(tot - cum)).astype(BF16)
                v = p_ref[bi, rows, 2 * dk_total + h * dv:2 * dk_total + (h + 1) * dv]
                upd = lax.dot_general(v, kd, (((0,), (0,)), ((), ())), preferred_element_type=F32)
                st = jnp.exp(tot) * st_ref[bi, h] + upd
                st_ref[bi, h] = st
                q = p_ref[bi, rows, ks]
                o = lax.dot_general(q, st.astype(BF16), (((1,), (1,)), ((), ())), preferred_element_type=F32)
                o = o * lax.rsqrt(jnp.mean(o * o, axis=-1, keepdims=True) + EPS) * gon_ref[...]
                g0 = 2 * dk_total + dv_total + h * dv
                g = p_ref[bi, rows, g0:g0 + dv].astype(F32)
                o_ref[bi, rows, vs] = (o * _silu(g)).astype(BF16)
        return carry

    lax.fori_loop(0, tc // CHUNK, chunk, 0)


def _gla_mixer(x, g_mix, mod, w_in, w_gk1, w_gk2, b_gk, g_onorm, w_out, tiles):
    b, l, d = x.shape
    tm, tc = tiles["row"], tiles["gla"]
    rank = w_gk1.shape[1]
    dk_total = w_gk2.shape[1]
    dv_total = w_out.shape[0]
    dk, dv = dk_total // GLA_HEADS, dv_total // GLA_HEADS
    n = w_in.shape[1]
    w1 = jnp.zeros((d, LANES), BF16).at[:, :rank].set(w_gk1.astype(BF16))
    w2 = jnp.zeros((LANES, dk_total), BF16).at[:rank, :].set(w_gk2.astype(BF16))
    p, la = pl.pallas_call(
        functools.partial(_gla_proj_kernel, dk_total=dk_total, qscale=dk ** -0.5),
        out_shape=(jax.ShapeDtypeStruct((b, l, n), BF16), jax.ShapeDtypeStruct((b, l, dk_total), F32)),
        grid=(b, l // tm),
        in_specs=[_row_spec(tm, d), _const_spec((1, d)), _mod_spec(d, 1), _mod_spec(d, 0),
                  _const_spec((d, n)), _const_spec((d, LANES)), _const_spec((LANES, dk_total)),
                  _const_spec((1, dk_total))],
        out_specs=(_row_spec(tm, n), _row_spec(tm, dk_total)),
        compiler_params=_params(("parallel", "parallel")),
        name="gla_proj",
    )(x, g_mix.reshape(1, d), mod, mod, w_in.astype(BF16), w1, w2, b_gk.reshape(1, dk_total))
    nb = tiles["gla_batch"] if b % tiles["gla_batch"] == 0 else 1

    def scan_spec(width):
        return pl.BlockSpec((nb, tc, width), lambda bi, ti: (bi, ti, 0))

    o = pl.pallas_call(
        functools.partial(_gla_scan_kernel, nb=nb, tc=tc, dk=dk, dv=dv),
        out_shape=jax.ShapeDtypeStruct((b, l, dv_total), BF16),
        grid=(b // nb, l // tc),
        in_specs=[scan_spec(n), scan_spec(dk_total), _const_spec((1, dv))],
        out_specs=scan_spec(dv_total),
        scratch_shapes=[pltpu.VMEM((nb, GLA_HEADS, dv, dk), F32)],
        compiler_params=_params(("arbitrary", "arbitrary")),
        name="gla_scan",
    )(p, la, g_onorm.reshape(1, dv))
    return _proj_res(_proj_res_kernel, o, w_out.astype(BF16), x, mod, 2, tm, "gla_out")


def _mlstm_kernel(xz_ref, cw_ref, cb_ref, wq_ref, wk_ref, wv_ref, wif_ref, bif_ref, skip_ref, gn_ref,
                  o_ref, ext_ref, q_ref, k_ref, v_ref, xc_ref, c_ref, n_ref, m_ref, *, tc, inner):
    heads = MLSTM_HEADS
    dh = inner // heads
    conv_w = cw_ref.shape[0]
    halo = SUBLANES

    @pl.when(pl.program_id(1) == 0)
    def _():
        ext_ref[0:halo, :] = jnp.zeros((halo, inner), F32)
        c_ref[...] = jnp.zeros_like(c_ref)
        n_ref[...] = jnp.zeros_like(n_ref)
        m_ref[...] = jnp.zeros_like(m_ref)

    ext_ref[halo:halo + tc, :] = xz_ref[:, :inner].astype(F32)
    acc = jnp.broadcast_to(cb_ref[...], (tc, inner))
    for j in range(conv_w):
        off = halo - (conv_w - 1) + j
        acc = acc + cw_ref[j:j + 1, :] * ext_ref[off:off + tc, :]
    ext_ref[0:halo, :] = ext_ref[tc:tc + halo, :]
    xc = _silu(acc)
    xc_ref[...] = xc

    gates = jnp.broadcast_to(bif_ref[...], (tc, LANES))
    for i in range(inner // MXU):
        sl = slice(i * MXU, (i + 1) * MXU)
        xcb = xc[:, sl].astype(BF16)
        q = jnp.dot(xcb, wq_ref[i], preferred_element_type=F32)
        k = jnp.dot(xcb, wk_ref[i], preferred_element_type=F32)
        v = jnp.dot(xz_ref[:, sl], wv_ref[i], preferred_element_type=F32)
        qb, kb, vb = q.astype(BF16), k.astype(BF16), v.astype(BF16)
        q_ref[:, sl] = qb
        k_ref[:, sl] = (k * dh ** -0.5).astype(BF16)
        v_ref[:, sl] = vb
        gates = gates + jnp.dot(qb, wif_ref[i * MXU:(i + 1) * MXU, :], preferred_element_type=F32)
        gates = gates + jnp.dot(kb, wif_ref[inner + i * MXU:inner + (i + 1) * MXU, :], preferred_element_type=F32)
        gates = gates + jnp.dot(vb, wif_ref[2 * inner + i * MXU:2 * inner + (i + 1) * MXU, :],
                                preferred_element_type=F32)
    rr = lax.broadcasted_iota(jnp.int32, (tc, tc), 0)
    cc = lax.broadcasted_iota(jnp.int32, (tc, tc), 1)
    cum = jnp.dot((rr >= cc).astype(F32), _log_sigmoid(gates), preferred_element_type=F32, precision=HIGHEST)
    a_all = gates - pltpu.roll(cum, LANES - heads, 1)
    a_rows = a_all.T
    chunk_r = lax.broadcasted_iota(jnp.int32, (tc, 1), 0) // CHUNK
    chunk_c = lax.broadcasted_iota(jnp.int32, (1, tc), 1) // CHUNK
    visible = chunk_c <= chunk_r
    n_chunks = tc // CHUNK
    for h in range(heads):
        hs = slice(h * dh, (h + 1) * dh)
        a_col = a_all[:, h:h + 1]
        f_col = cum[:, heads + h:heads + h + 1]
        m_in = m_ref[h][:, 0:1]
        mu = m_in
        mu_col = jnp.zeros((tc, 1), F32)
        f_end = jnp.zeros((tc, 1), F32)
        for c in range(n_chunks):
            mu = jnp.maximum(mu, jnp.max(jnp.where(chunk_r == c, a_col, NEG_BIG), axis=0, keepdims=True))
            mu_col = jnp.where(chunk_r == c, mu, mu_col)
            f_end = jnp.where(chunk_r == c, f_col[(c + 1) * CHUNK - 1:(c + 1) * CHUNK, :], f_end)
        q = q_ref[:, hs]
        k = k_ref[:, hs]
        v = v_ref[:, hs]
        w = jnp.where(visible, jnp.exp(a_rows[h:h + 1, :] - mu_col), 0.0)
        p = lax.dot_general(q, k, (((1,), (1,)), ((), ())), preferred_element_type=F32) * w
        carry_in = jnp.exp(m_in - mu_col)
        num = jnp.dot(p.astype(BF16), v, preferred_element_type=F32) \
            + carry_in * jnp.dot(q, c_ref[h].astype(BF16), preferred_element_type=F32)
        qn = carry_in * jnp.sum(q.astype(F32) * n_ref[h], axis=-1, keepdims=True) + jnp.sum(p, axis=-1, keepdims=True)
        den = jnp.maximum(jnp.abs(qn), jnp.exp(-(mu_col + f_end)))
        hc = num / den
        mean = jnp.mean(hc, axis=-1, keepdims=True)
        var = jnp.mean((hc - mean) * (hc - mean), axis=-1, keepdims=True)
        hn = (hc - mean) * lax.rsqrt(var + EPS) * gn_ref[:, hs]
        z = xz_ref[:, inner + h * dh:inner + (h + 1) * dh].astype(F32)
        o_ref[:, hs] = ((hn + skip_ref[:, hs] * xc_ref[:, hs]) * _silu(z)).astype(BF16)
        decay = jnp.exp(m_in - mu)
        kw = k.astype(F32) * jnp.exp(a_col - mu)
        c_ref[h] = decay * c_ref[h] + lax.dot_general(kw.astype(BF16), v, (((0,), (0,)), ((), ())),
                                                      preferred_element_type=F32)
        n_ref[h] = decay * n_ref[h] + jnp.sum(kw, axis=0, keepdims=True)
        m_ref[h] = jnp.broadcast_to(mu + f_col[tc - 1:tc, :], (1, LANES))


def _diag_tiles(w):
    nblk, bs, _ = w.shape
    per = MXU // bs
    wt = w.reshape(nblk // per, per, bs, bs)
    eye = jnp.eye(per, dtype=w.dtype)
    return jnp.einsum("tpio,pq->tpiqo", wt, eye).reshape(nblk // per, MXU, MXU).astype(BF16)


def _mlstm_mixer(x, g_mix, mod, w_up, conv_w, conv_b, w_q, w_k, w_v, w_if, b_if, skip, g_norm, w_down, tiles):
    b, l, d = x.shape
    tm, tc = tiles["row"], tiles["mlstm"]
    inner = w_down.shape[0]
    heads = MLSTM_HEADS
    xz = _norm_proj(x, g_mix, mod, 1, 0, w_up.astype(BF16), BF16, tm)
    wif = jnp.zeros((3 * inner, LANES), BF16).at[:, :2 * heads].set(w_if.astype(BF16))
    bif = jnp.zeros((1, LANES), F32).at[0, :2 * heads].set(b_if)
    nt = inner // MXU
    y = pl.pallas_call(
        functools.partial(_mlstm_kernel, tc=tc, inner=inner),
        out_shape=jax.ShapeDtypeStruct((b, l, inner), BF16),
        grid=(b, l // tc),
        in_specs=[_row_spec(tc, 2 * inner), _const_spec(conv_w.shape), _const_spec((1, inner)),
                  _const_spec((nt, MXU, MXU)), _const_spec((nt, MXU, MXU)), _const_spec((nt, MXU, MXU)),
                  _const_spec((3 * inner, LANES)), _const_spec((1, LANES)), _const_spec((1, inner)),
                  _const_spec((1, inner))],
        out_specs=_row_spec(tc, inner),
        scratch_shapes=[pltpu.VMEM((tc + 2 * SUBLANES, inner), F32),
                        pltpu.VMEM((tc, inner), BF16), pltpu.VMEM((tc, inner), BF16),
                        pltpu.VMEM((tc, inner), BF16), pltpu.VMEM((tc, inner), F32),
                        pltpu.VMEM((heads, inner // heads, inner // heads), F32),
                        pltpu.VMEM((heads, 1, inner // heads), F32),
                        pltpu.VMEM((heads, 1, LANES), F32)],
        compiler_params=_params(("arbitrary", "arbitrary")),
        name="mlstm_scan",
    )(xz, conv_w, conv_b.reshape(1, inner), _diag_tiles(w_q), _diag_tiles(w_k), _diag_tiles(w_v), wif, bif,
      skip.reshape(1, inner), g_norm.reshape(1, inner))
    return _proj_res(_proj_res_kernel, y, w_down.astype(BF16), x, mod, 2, tm, "mlstm_out")


def _s5_kernel(u_ref, bt_ref, ct_ref, are_ref, aim_ref, d_ref, y_ref, bu_ref, st_ref, *, ts, nb, d, npart, per):
    rows = ts * nb
    half = per // 2
    ntile = npart // MXU

    @pl.when(pl.program_id(0) == 0)
    def _():
        st_ref[...] = jnp.zeros_like(st_ref)

    def col_tile(t, j):
        return half * t + j if j < half else ntile + half * t + (j - half)

    u = u_ref[...].reshape(rows, d)
    ub = u.astype(BF16)
    for kt in range(d // MXU):
        lhs = ub[:, kt * MXU:(kt + 1) * MXU]
        for j in range(per):
            n = col_tile(kt, j)
            bu_ref[:, n * MXU:(n + 1) * MXU] = jnp.dot(lhs, bt_ref[kt * per + j], preferred_element_type=F32)

    cb = min(512, npart)
    for c in range(npart // cb):
        re = slice(c * cb, (c + 1) * cb)
        im = slice(npart + c * cb, npart + (c + 1) * cb)
        ar = are_ref[:, re]
        ai = aim_ref[:, re]

        def step(t, carry):
            xr, xi = carry
            r = pl.ds(pl.multiple_of(t * nb, nb), nb)
            nr = ar * xr - ai * xi + bu_ref[r, re]
            ni = ar * xi + ai * xr + bu_ref[r, im]
            bu_ref[r, re] = nr
            bu_ref[r, im] = ni
            return nr, ni

        xr, xi = lax.fori_loop(0, ts, step, (st_ref[:, re], st_ref[:, im]), unroll=4)
        st_ref[:, re] = xr
        st_ref[:, im] = xi

    for nt in range(d // MXU):
        acc = jnp.zeros((rows, MXU), F32)
        for j in range(per):
            n = col_tile(nt, j)
            acc = acc + jnp.dot(bu_ref[:, n * MXU:(n + 1) * MXU].astype(BF16), ct_ref[nt * per + j],
                                preferred_element_type=F32)
        ch = slice(nt * MXU, (nt + 1) * MXU)
        yv = acc + d_ref[:, ch] * u[:, ch]
        y_ref[:, :, ch] = jax.nn.gelu(yv).reshape(ts, nb, MXU)


def _s5_mixer(xs, g_mix, mods, w_in, a_re, a_im, log_dt, b_re, b_im, c_re, c_im, d_skip, w_out, tiles):
    l, d = xs[0].shape[1:]
    b = sum(p.shape[0] for p in xs)
    tm, ts = tiles["row"], tiles["s5"]
    groups, pst = a_re.shape
    npart = groups * pst
    gk = MXU // S5_GROUP
    gn = MXU // pst
    per = 2 * (gk // gn)
    nkt = d // MXU

    a = lax.complex(a_re, a_im)
    dt = jnp.exp(log_dt)[:, None]
    a_bar = jnp.exp(a * dt)
    b_bar = ((a_bar - 1.0) / a)[..., None] * lax.complex(b_re, b_im)
    eye = jnp.eye(groups, dtype=F32)
    bfull = jnp.concatenate([jnp.einsum("gpc,gh->gchp", part, eye).reshape(d, npart)
                             for part in (jnp.real(b_bar), jnp.imag(b_bar))], axis=1)
    cfull = jnp.concatenate([jnp.einsum("gcp,gh->gphc", part, eye).reshape(npart, d)
                             for part in (c_re, -c_im)], axis=0)
    half = per // 2
    ntile = npart // MXU

    def col_tile(t, j):
        return half * t + j if j < half else ntile + half * t + (j - half)

    bt = jnp.stack([bfull[kt * MXU:(kt + 1) * MXU, col_tile(kt, j) * MXU:(col_tile(kt, j) + 1) * MXU]
                    for kt in range(nkt) for j in range(per)]).astype(BF16)
    ct = jnp.stack([cfull[col_tile(nt, j) * MXU:(col_tile(nt, j) + 1) * MXU, nt * MXU:(nt + 1) * MXU]
                    for nt in range(nkt) for j in range(per)]).astype(BF16)
    are = jnp.broadcast_to(jnp.real(a_bar).reshape(1, npart), (b, npart))
    aim = jnp.broadcast_to(jnp.imag(a_bar).reshape(1, npart), (b, npart))

    win = w_in.astype(BF16)
    ut, b0 = None, 0
    for x, mod in zip(xs, mods):
        ut = _norm_proj_time_major(x, g_mix, mod, 1, 0, win, tm, ut, b0, b)
        b0 += x.shape[0]
    ut = ut.reshape(l, b, d)
    yt = pl.pallas_call(
        functools.partial(_s5_kernel, ts=ts, nb=b, d=d, npart=npart, per=per),
        out_shape=jax.ShapeDtypeStruct((l, b, d), F32),
        grid=(l // ts,),
        in_specs=[pl.BlockSpec((ts, b, d), lambda t: (t, 0, 0)),
                  pl.BlockSpec((nkt * per, MXU, MXU), lambda t: (0, 0, 0)),
                  pl.BlockSpec((nkt * per, MXU, MXU), lambda t: (0, 0, 0)),
                  pl.BlockSpec((b, npart), lambda t: (0, 0)),
                  pl.BlockSpec((b, npart), lambda t: (0, 0)),
                  pl.BlockSpec((1, d), lambda t: (0, 0))],
        out_specs=pl.BlockSpec((ts, b, d), lambda t: (t, 0, 0)),
        scratch_shapes=[pltpu.VMEM((ts * b, 2 * npart), F32), pltpu.VMEM((b, 2 * npart), F32)],
        compiler_params=_params(("arbitrary",)),
        name="s5_scan",
    )(ut, bt, ct, are, aim, d_skip.reshape(1, d))
    yt = yt.reshape(l, b * d)
    wout = w_out.astype(BF16)
    outs, b0 = [], 0
    for x, mod in zip(xs, mods):
        outs.append(_proj_res(_glu_res_kernel, yt, wout, x, mod, 2, tm, "s5_out", y_boff=b0))
        b0 += x.shape[0]
    return outs


def _router_kernel(x_ref, g_ref, sc_ref, sh_ref, wr_ref, br_ref, h_ref, ri_ref, rw_ref, cnt_ref, run_ref, *, tm):
    @pl.when((pl.program_id(0) == 0) & (pl.program_id(1) == 0))
    def _():
        run_ref[...] = jnp.zeros_like(run_ref)

    hb = _normmod(x_ref[...], g_ref[...], sc_ref[...], sh_ref[...]).astype(BF16)
    h_ref[...] = hb
    logits = lax.dot_general(wr_ref[...], hb, (((1,), (1,)), ((), ())), preferred_element_type=F32) + br_ref[...]
    n_exp = logits.shape[0]
    erow = lax.broadcasted_iota(jnp.int32, logits.shape, 0).astype(F32)
    work = logits
    vals, idxs = [], []
    for _ in range(TOP_K):
        m = jnp.max(work, axis=0, keepdims=True)
        idx = jnp.min(jnp.where(work == m, erow, float(n_exp)), axis=0, keepdims=True)
        vals.append(m)
        idxs.append(idx)
        work = jnp.where(erow == idx, NEG_BIG, work)
    exps = [jnp.exp(v - vals[0]) for v in vals]
    denom = exps[0]
    for e in exps[1:]:
        denom = denom + e
    multi = jnp.zeros(logits.shape, F32)
    for idx in idxs:
        multi = multi + (erow == idx).astype(F32)
    lane = lax.broadcasted_iota(jnp.int32, logits.shape, 1)
    inc = multi
    shift = 1
    while shift < tm:
        inc = inc + jnp.where(lane >= shift, pltpu.roll(inc, shift, 1), 0.0)
        shift *= 2
    run = run_ref[:, 0:1]
    cs = inc - multi + run
    run = run + inc[:, tm - 1:tm]
    run_ref[...] = jnp.broadcast_to(run, run_ref.shape)
    cnt_ref[...] = jnp.broadcast_to(run, cnt_ref.shape).astype(jnp.int32)
    srow = lax.broadcasted_iota(jnp.int32, (2 * TOP_K, tm), 0)
    ri = jnp.zeros((2 * TOP_K, tm), F32)
    rw = jnp.zeros((2 * TOP_K, tm), F32)
    for k in range(TOP_K):
        rank = jnp.sum(jnp.where(erow == idxs[k], cs, 0.0), axis=0, keepdims=True)
        ri = ri + jnp.where(srow == k, idxs[k], 0.0) + jnp.where(srow == TOP_K + k, rank, 0.0)
        rw = rw + jnp.where(srow == k, exps[k] / denom, 0.0)
    ri_ref[...] = ri.astype(jnp.int32)
    rw_ref[...] = jnp.concatenate([rw, jnp.zeros((LANES - 2 * TOP_K, tm), F32)], axis=0).T


def _dest_kernel(offs_ref, ri_ref, o_ref, *, n_exp):
    idx = ri_ref[0:TOP_K, :]
    off = jnp.zeros(idx.shape, jnp.int32)
    for e in range(n_exp):
        off = jnp.where(idx == e, offs_ref[e], off)
    o_ref[0:TOP_K, :] = off + ri_ref[TOP_K:2 * TOP_K, :]
    o_ref[TOP_K:2 * TOP_K, :] = idx


def _expert_kernel(te_ref, nv_ref, xs_ref, wgu_ref, bgu_ref, wd_ref, bd_ref, o_ref, *rest, dff, cast):
    i = pl.program_id(0)
    valid = i < nv_ref[0]
    if cast:
        wgu_bf, wd_bf, act_ref = rest
        cast_rows = 2 * SUBLANES * SUBLANES

        @pl.when(valid & ((i == 0) | (te_ref[i] != te_ref[jnp.maximum(i - 1, 0)])))
        def _():
            def cast_gu(r, carry):
                rows = pl.ds(pl.multiple_of(r * cast_rows, cast_rows), cast_rows)
                wgu_bf[rows, :] = wgu_ref[rows, :].astype(BF16)
                return carry

            def cast_d(r, carry):
                rows = pl.ds(pl.multiple_of(r * cast_rows, cast_rows), cast_rows)
                wd_bf[rows, :] = wd_ref[rows, :].astype(BF16)
                return carry

            lax.fori_loop(0, wgu_ref.shape[0] // cast_rows, cast_gu, 0)
            lax.fori_loop(0, wd_ref.shape[0] // cast_rows, cast_d, 0)
    else:
        wgu_bf, wd_bf = wgu_ref, wd_ref
        (act_ref,) = rest

    @pl.when(valid)
    def _():
        x = xs_ref[...]
        gate = jnp.dot(x, wgu_bf[:, :dff], preferred_element_type=F32) + bgu_ref[:, :dff]
        up = jnp.dot(x, wgu_bf[:, dff:], preferred_element_type=F32) + bgu_ref[:, dff:]
        gate = jnp.minimum(gate, SWIGLU_LIMIT)
        up = jnp.clip(up, -SWIGLU_LIMIT, SWIGLU_LIMIT)
        act_ref[...] = ((up + 1.0) * gate * jax.nn.sigmoid(SWIGLU_ALPHA * gate)).astype(BF16)
        o_ref[...] = (jnp.dot(act_ref[...], wd_bf[...], preferred_element_type=F32) + bd_ref[...]).astype(o_ref.dtype)

    @pl.when(jnp.logical_not(valid))
    def _():
        o_ref[...] = jnp.zeros_like(o_ref)


def _combine_kernel(y0_ref, y1_ref, y2_ref, y3_ref, rw_ref, x_ref, gt_ref, gf_ref, *rest, final):
    o_ref = rest[-1]
    acc = jnp.zeros(o_ref.shape, F32)
    for k, y_ref in enumerate((y0_ref, y1_ref, y2_ref, y3_ref)):
        acc = acc + rw_ref[:, k:k + 1] * y_ref[...].astype(F32)
    out = x_ref[...] + gt_ref[...] * acc
    if final:
        out = out * lax.rsqrt(jnp.mean(out * out, axis=-1, keepdims=True) + EPS) * gf_ref[...]
    o_ref[...] = out


def _moe_route(x, g_ffn, mod, w_router, b_router, tiles):
    b, l, d = x.shape
    t = b * l
    tm, te = tiles["row"], tiles["expert"]
    n_exp = w_router.shape[1]
    assert TOP_K == 4
    wr = w_router.T.astype(BF16)
    br = b_router.reshape(n_exp, 1)
    nt = l // tm
    h2, ri, rw, cnt = pl.pallas_call(
        functools.partial(_router_kernel, tm=tm),
        out_shape=(jax.ShapeDtypeStruct((b, l, d), BF16), jax.ShapeDtypeStruct((2 * TOP_K, t), jnp.int32),
                   jax.ShapeDtypeStruct((b, l, LANES), F32), jax.ShapeDtypeStruct((n_exp, LANES), jnp.int32)),
        grid=(b, nt),
        in_specs=[_row_spec(tm, d), _const_spec((1, d)), _mod_spec(d, 4), _mod_spec(d, 3),
                  _const_spec((n_exp, d)), _const_spec((n_exp, 1))],
        out_specs=(_row_spec(tm, d), pl.BlockSpec((2 * TOP_K, tm), lambda bi, ti: (0, bi * nt + ti)),
                   _row_spec(tm, LANES), _const_spec((n_exp, LANES))),
        scratch_shapes=[pltpu.VMEM((n_exp, LANES), F32)],
        compiler_params=_params(("arbitrary", "arbitrary")),
        name="moe_router",
    )(x, g_ffn.reshape(1, d), mod, mod, wr, br)

    counts = cnt[:, 0]
    padded = jnp.maximum((counts + te - 1) // te, 1) * te
    ends = jnp.cumsum(padded)
    offs = ends - padded
    n_rows = t * TOP_K + n_exp * te
    n_tiles = n_rows // te
    tcol = min(t, 8192)
    dest = pl.pallas_call(
        functools.partial(_dest_kernel, n_exp=n_exp),
        out_shape=jax.ShapeDtypeStruct((2 * TOP_K, t), jnp.int32),
        grid_spec=pltpu.PrefetchScalarGridSpec(
            num_scalar_prefetch=1,
            grid=(t // tcol,),
            in_specs=[pl.BlockSpec((2 * TOP_K, tcol), lambda i, o: (0, i))],
            out_specs=pl.BlockSpec((2 * TOP_K, tcol), lambda i, o: (0, i))),
        compiler_params=_params(("parallel",)),
        name="moe_dest",
    )(offs.astype(jnp.int32), ri)[:TOP_K]
    slot = jnp.arange(te, dtype=jnp.int32)[None, :]
    n_pad = (padded - counts)[:, None]
    n_tail = te - n_pad
    tail_base = ends[-1] + (jnp.cumsum(n_tail[:, 0]) - n_tail[:, 0])[:, None]
    free_rows = jnp.where(slot < n_pad, (offs + counts)[:, None] + slot, tail_base + slot - n_pad)
    filler = ((jnp.arange(n_exp * te, dtype=jnp.int32) * 61) % t).reshape(n_exp, te)
    filler = jnp.where((slot == 0) & (counts[:, None] % 2 == 1), t - 1, filler)
    keys = jnp.concatenate([dest.reshape(-1), free_rows.reshape(-1).astype(jnp.int32)])
    vals = jnp.concatenate([jnp.tile(jnp.arange(t, dtype=jnp.int32), TOP_K), filler.reshape(-1)])
    tok_bits = (t - 1).bit_length()
    assert (((n_rows // 2 - 1) << tok_bits) | (t - 1)) < 2 ** 32 and te % 2 == 0
    packed = ((keys >> 1).astype(jnp.uint32) << tok_bits) | vals.astype(jnp.uint32)
    src = (jnp.sort(packed) & jnp.uint32((1 << tok_bits) - 1)).astype(jnp.int32)
    tile_start = jnp.arange(n_tiles, dtype=jnp.int32) * te
    tile_exp = jnp.minimum(jnp.sum(tile_start[:, None] >= ends[None, :], axis=1), n_exp - 1).astype(jnp.int32)
    n_valid = (ends[-1:] // te).astype(jnp.int32)
    return dict(h2=h2, src=src, dest=dest, rw=rw, tile_exp=tile_exp, n_valid=n_valid)


def _moe_dispatch(route):
    h2 = route["h2"]
    return h2.reshape(-1, h2.shape[-1]).at[route["src"]].get(mode="promise_in_bounds")


def _moe_experts(route, xs, w_gu, b_gu, w_down, b_down, tiles, layer=None):
    n_rows, d = xs.shape
    te = tiles["expert"]
    cast = layer is not None
    n_exp, dff = w_down.shape[-3:-1]
    n_tiles = n_rows // te
    if cast:
        def wspec(r, c):
            return pl.BlockSpec((None, None, r, c), lambda i, e, nv: (layer, e[i], 0, 0))
        b_gu, b_down = b_gu[layer], b_down[layer]
    else:
        def wspec(r, c):
            return pl.BlockSpec((None, r, c), lambda i, e, nv: (e[i], 0, 0))

    def espec(r, c):
        return pl.BlockSpec((None, r, c), lambda i, e, nv: (e[i], 0, 0))

    rows_spec = pl.BlockSpec((te, d), lambda i, e, nv: (i, 0))
    out_shape = [jax.ShapeDtypeStruct((n_rows, d), BF16)]
    out_specs = [rows_spec]
    if cast:
        out_shape += [jax.ShapeDtypeStruct((n_exp, d, 2 * dff), BF16), jax.ShapeDtypeStruct((n_exp, dff, d), BF16)]
        out_specs += [espec(d, 2 * dff), espec(dff, d)]
    res = pl.pallas_call(
        functools.partial(_expert_kernel, dff=dff, cast=cast),
        out_shape=tuple(out_shape),
        grid_spec=pltpu.PrefetchScalarGridSpec(
            num_scalar_prefetch=2,
            grid=(n_tiles,),
            in_specs=[rows_spec, wspec(d, 2 * dff), espec(1, 2 * dff), wspec(dff, d), espec(1, d)],
            out_specs=tuple(out_specs),
            scratch_shapes=[pltpu.VMEM((te, dff), BF16)]),
        compiler_params=_params(("arbitrary",)),
        name="moe_experts_cast" if cast else "moe_experts",
    )(route["tile_exp"], route["n_valid"], xs, w_gu, b_gu.reshape(n_exp, 1, 2 * dff), w_down,
      b_down.reshape(n_exp, 1, d))
    return res if cast else res[0]


def _moe_gather_back(route, ys):
    return [ys.at[route["dest"][k]].get(mode="promise_in_bounds") for k in range(TOP_K)]


def _moe_combine(x, yk, route, mod, g_final, final, tiles, out=None):
    b, l, d = x.shape
    t = b * l
    tm = tiles["row"]
    yk = [y.reshape(b, l, d) for y in yk]
    rw = route["rw"]
    in_specs = [_row_spec(tm, d)] * TOP_K + [_row_spec(tm, LANES), _row_spec(tm, d), _mod_spec(d, 5),
                                             _const_spec((1, d))]
    args = [*yk, rw, x, mod, g_final.reshape(1, d)]
    buf, boff, b_total = (None, 0, b) if out is None else out
    aliases = {}
    if buf is not None:
        in_specs.append(pl.BlockSpec(memory_space=pl.ANY))
        args.append(buf)
        aliases = {len(args) - 1: 0}
    return pl.pallas_call(
        functools.partial(_combine_kernel, final=final),
        out_shape=jax.ShapeDtypeStruct((b_total, l, d), F32),
        grid=(b, l // tm),
        in_specs=in_specs,
        out_specs=pl.BlockSpec((None, tm, d), lambda bi, ti: (boff + bi, ti, 0)),
        input_output_aliases=aliases,
        compiler_params=_params(("parallel", "parallel")),
        name="moe_combine",
    )(*args)


def kernel(x, c, g_mix, g_ffn, w_ada, b_ada, gla_w_in, gla_w_gk1, gla_w_gk2, gla_b_gk, gla_g_onorm, gla_w_out, ml_w_up, ml_conv_w, ml_conv_b, ml_w_q, ml_w_k, ml_w_v, ml_w_if, ml_b_if, ml_skip, ml_g_norm, ml_w_down, s5_w_in, s5_a_re, s5_a_im, s5_log_dt, s5_b_re, s5_b_im, s5_c_re, s5_c_im, s5_d, s5_w_out, moe_w_router, moe_b_router, moe_w_gu, moe_b_gu, moe_w_down, moe_b_down, g_final):
    b, l, d = x.shape
    depth = w_ada.shape[0]
    n_exp = moe_w_router.shape[-1]
    assert l % CHUNK == 0 and d % MXU == 0
    n_parts = 2 if b % 2 == 0 else 1
    bp = b // n_parts
    tiles = _tiles(l, bp * l, n_exp)
    mods = _modulation(c, w_ada, b_ada)
    xs = [x[s * bp:(s + 1) * bp] for s in range(n_parts)]
    after = lax.optimization_barrier
    for i in range(depth):
        mp = [mods[i, s * bp:(s + 1) * bp].reshape(bp, 6, 1, d) for s in range(n_parts)]
        kind, j = i % N_MIXERS, i // N_MIXERS
        last = i == depth - 1

        def mixer(xp, mod):
            if kind == 0:
                return _gla_mixer(xp, g_mix[i], mod, gla_w_in[j], gla_w_gk1[j], gla_w_gk2[j], gla_b_gk[j],
                                  gla_g_onorm[j], gla_w_out[j], tiles)
            return _mlstm_mixer(xp, g_mix[i], mod, ml_w_up[j], ml_conv_w[j], ml_conv_b[j], ml_w_q[j], ml_w_k[j],
                                ml_w_v[j], ml_w_if[j], ml_b_if[j], ml_skip[j], ml_g_norm[j], ml_w_down[j], tiles)

        def route(xp, mod):
            return _moe_route(xp, g_ffn[i], mod, moe_w_router[i], moe_b_router[i], tiles)

        def experts_casting(r, rows):
            return _moe_experts(r, rows, moe_w_gu, moe_b_gu, moe_w_down, moe_b_down, tiles, layer=i)

        joint = kind == 2
        if joint:
            xs = _s5_mixer(xs, g_mix[i], mp, s5_w_in[j], s5_a_re[j], s5_a_im[j], s5_log_dt[j], s5_b_re[j],
                           s5_b_im[j], s5_c_re[j], s5_c_im[j], s5_d[j], s5_w_out[j], tiles)
        if n_parts == 1:
            x0 = xs[0] if joint else mixer(xs[0], mp[0])
            r0 = route(x0, mp[0])
            y0 = _moe_gather_back(r0, experts_casting(r0, _moe_dispatch(r0))[0])
            xs = [_moe_combine(x0, y0, r0, mp[0], g_final, last, tiles)]
            continue
        xa, xb = xs
        xa = xa if joint else mixer(xa, mp[0])
        ra = route(xa, mp[0])
        ra["src"], xb = after((ra["src"], xb))
        rows_a = _moe_dispatch(ra)
        xb = xb if joint else mixer(xb, mp[1])
        rb = route(xb, mp[1])
        rows_a, rb["src"] = after((rows_a, rb["src"]))
        rows_b = _moe_dispatch(rb)
        ys_a, wgu, wdn = experts_casting(ra, rows_a)
        back_a = _moe_gather_back(ra, ys_a)
        ys_b = _moe_experts(rb, rows_b, wgu, moe_b_gu[i], wdn, moe_b_down[i], tiles)
        back_a, ys_b = after((back_a, ys_b))
        back_b = _moe_gather_back(rb, ys_b)
        xa = _moe_combine(xa, back_a, ra, mp[0], g_final, last, tiles, (None, 0, b) if last else None)
        xb = _moe_combine(xb, back_b, rb, mp[1], g_final, last, tiles, (xa, bp, b) if last else None)
        xs = [xa, xb]
    return xs[-1] if n_parts == 2 else xs[0]
```

```python
import functools

import jax
import jax.numpy as jnp
from jax import lax
from jax.experimental import pallas as pl
from jax.experimental.pallas import tpu as pltpu

F32 = jnp.float32
BF16 = jnp.bfloat16
HIGHEST = lax.Precision.HIGHEST

EPS = 1e-6
CHUNK = 64
N_MIXERS = 3
GLA_HEADS = 4
GLA_GATE_NORM = 16.0
MLSTM_HEADS = 4
S5_GROUP = 16
TOP_K = 4
SWIGLU_LIMIT = 7.0
SWIGLU_ALPHA = 1.702

LANES = 128
SUBLANES = 8
MXU = 256
VMEM_LIMIT_BYTES = 56 * 1024 * 1024

NEG_BIG = -1e30


def _tiles(seq, tokens, n_experts):
    row = min(1024, seq)
    gla = min(256, seq)
    gla_batch = 4
    mlstm = min(256, seq)
    s5 = min(64, seq)
    per_expert = tokens * TOP_K // n_experts
    expert = 1024 if per_expert >= 2048 else 128
    return dict(row=row, gla=gla, gla_batch=gla_batch, mlstm=mlstm, s5=s5, expert=expert)


def _params(sem):
    return pltpu.CompilerParams(dimension_semantics=sem, vmem_limit_bytes=VMEM_LIMIT_BYTES)


def _normmod(x, g, sc, sh):
    y = x * lax.rsqrt(jnp.mean(x * x, axis=-1, keepdims=True) + EPS)
    return (y * g) * (1.0 + sc) + sh


def _log_sigmoid(x):
    return jnp.minimum(x, 0.0) - jnp.log(1.0 + jnp.exp(-jnp.abs(x)))


def _silu(x):
    return x * jax.nn.sigmoid(x)


def _mod_spec(d, j):
    return pl.BlockSpec((None, None, 1, d), lambda b, t: (b, j, 0, 0))


def _row_spec(tm, n):
    return pl.BlockSpec((None, tm, n), lambda b, t: (b, t, 0))


def _const_spec(shape):
    nd = len(shape)
    return pl.BlockSpec(shape, lambda b, t: (0,) * nd)


def _mod_kernel(c_ref, w_ref, b_ref, o_ref):
    ca = _silu(c_ref[...]).astype(BF16)
    o_ref[0] = jnp.dot(ca, w_ref[0].astype(BF16), preferred_element_type=F32) + b_ref[0]


def _modulation(c, w_ada, b_ada):
    depth, d, n = w_ada.shape
    b = c.shape[0]
    tn = n // 4
    return pl.pallas_call(
        _mod_kernel,
        out_shape=jax.ShapeDtypeStruct((depth, b, n), F32),
        grid=(depth, n // tn),
        in_specs=[pl.BlockSpec((b, d), lambda i, j: (0, 0)),
                  pl.BlockSpec((1, d, tn), lambda i, j: (i, 0, j)),
                  pl.BlockSpec((1, 1, tn), lambda i, j: (i, 0, j))],
        out_specs=pl.BlockSpec((1, b, tn), lambda i, j: (i, 0, j)),
        compiler_params=_params(("parallel", "parallel")),
        name="adaln_mod",
    )(c, w_ada, b_ada.reshape(depth, 1, n))


def _norm_proj_kernel(x_ref, g_ref, sc_ref, sh_ref, w_ref, o_ref):
    h = _normmod(x_ref[...], g_ref[...], sc_ref[...], sh_ref[...]).astype(BF16)
    o_ref[...] = jnp.dot(h, w_ref[...], preferred_element_type=F32).astype(o_ref.dtype)


def _norm_proj(x, g, mod, jsc, jsh, w, out_dtype, tm):
    b, l, d = x.shape
    n = w.shape[1]
    return pl.pallas_call(
        _norm_proj_kernel,
        out_shape=jax.ShapeDtypeStruct((b, l, n), out_dtype),
        grid=(b, l // tm),
        in_specs=[_row_spec(tm, d), _const_spec((1, d)), _mod_spec(d, jsc), _mod_spec(d, jsh),
                  _const_spec((d, n))],
        out_specs=_row_spec(tm, n),
        compiler_params=_params(("parallel", "parallel")),
        name="norm_proj",
    )(x, g.reshape(1, d), mod, mod, w)


def _norm_proj_tm_kernel(x_ref, g_ref, sc_ref, sh_ref, w_ref, *rest):
    _norm_proj_kernel(x_ref, g_ref, sc_ref, sh_ref, w_ref, rest[-1])


def _norm_proj_time_major(x, g, mod, jsc, jsh, w, tm, buf, boff, b_total):
    b, l, d = x.shape
    n = w.shape[1]
    in_specs = [_row_spec(tm, d), _const_spec((1, d)), _mod_spec(d, jsc), _mod_spec(d, jsh), _const_spec((d, n))]
    args = [x, g.reshape(1, d), mod, mod, w]
    aliases = {}
    if buf is not None:
        in_specs.append(pl.BlockSpec(memory_space=pl.ANY))
        args.append(buf)
        aliases = {len(args) - 1: 0}
    return pl.pallas_call(
        _norm_proj_tm_kernel,
        out_shape=jax.ShapeDtypeStruct((l, b_total * n), F32),
        grid=(b, l // tm),
        in_specs=in_specs,
        out_specs=pl.BlockSpec((tm, n), lambda bi, ti: (ti, boff + bi)),
        input_output_aliases=aliases,
        compiler_params=_params(("parallel", "parallel")),
        name="norm_proj_tm",
    )(*args)


def _proj_res_kernel(y_ref, w_ref, x_ref, gt_ref, o_ref):
    o_ref[...] = x_ref[...] + gt_ref[...] * jnp.dot(y_ref[...], w_ref[...], preferred_element_type=F32)


def _glu_res_kernel(y_ref, w_ref, x_ref, gt_ref, o_ref):
    d = o_ref.shape[-1]
    glu = jnp.dot(y_ref[...].astype(BF16), w_ref[...], preferred_element_type=F32)
    o_ref[...] = x_ref[...] + gt_ref[...] * (glu[:, :d] * jax.nn.sigmoid(glu[:, d:]))


def _proj_res(kernel, y, w, x, mod, jgt, tm, name, y_boff=None):
    b, l, d = x.shape
    k, n = w.shape
    y_spec = _row_spec(tm, k) if y_boff is None else pl.BlockSpec((tm, k), lambda bi, ti: (ti, y_boff + bi))
    return pl.pallas_call(
        kernel,
        out_shape=jax.ShapeDtypeStruct((b, l, d), F32),
        grid=(b, l // tm),
        in_specs=[y_spec, _const_spec((k, n)), _row_spec(tm, d), _mod_spec(d, jgt)],
        out_specs=_row_spec(tm, d),
        compiler_params=_params(("parallel", "parallel")),
        name=name,
    )(y, w, x, mod)


def _gla_proj_kernel(x_ref, g_ref, sc_ref, sh_ref, w_ref, w1_ref, w2_ref, b2_ref, p_ref, la_ref, *, dk_total, qscale):
    h = _normmod(x_ref[...], g_ref[...], sc_ref[...], sh_ref[...]).astype(BF16)
    proj = jnp.dot(h, w_ref[...], preferred_element_type=F32)
    p_ref[:, :dk_total] = (proj[:, :dk_total] * qscale).astype(BF16)
    p_ref[:, dk_total:] = proj[:, dk_total:].astype(BF16)
    t1 = jnp.dot(h, w1_ref[...], preferred_element_type=F32).astype(BF16)
    gk = jnp.dot(t1, w2_ref[...], preferred_element_type=F32) + b2_ref[...]
    la_ref[...] = _log_sigmoid(gk) * (1.0 / GLA_GATE_NORM)


def _gla_scan_kernel(p_ref, la_ref, gon_ref, o_ref, st_ref, *, nb, tc, dk, dv):
    heads = GLA_HEADS
    dk_total, dv_total = heads * dk, heads * dv

    @pl.when(pl.program_id(1) == 0)
    def _():
        st_ref[...] = jnp.zeros_like(st_ref)

    rr = lax.broadcasted_iota(jnp.int32, (CHUNK, CHUNK), 0)
    cc = lax.broadcasted_iota(jnp.int32, (CHUNK, CHUNK), 1)
    tri = (rr >= cc).astype(F32)

    def chunk(j, carry):
        rows = pl.ds(pl.multiple_of(j * CHUNK, CHUNK), CHUNK)
        for bi in range(nb):
            cum_all = jnp.dot(tri, la_ref[bi, rows, :], preferred_element_type=F32, precision=HIGHEST)
            for h in range(heads):
                ks = slice(h * dk, (h + 1) * dk)
                vs = slice(h * dv, (h + 1) * dv)
                cum = cum_all[:, ks]
                tot = cum[CHUNK - 1:CHUNK, :]
                kblk = p_ref[bi, rows, dk_total + h * dk:dk_total + (h + 1) * dk].astype(F32)
                kd = (kblk * jnp.exp(tot - cum)).astype(BF16)
                v = p_ref[bi, rows, 2 * dk_total + h * dv:2 * dk_total + (h + 1) * dv]
                upd = lax.dot_general(v, kd, (((0,), (0,)), ((), ())), preferred_element_type=F32)
                st = jnp.exp(tot) * st_ref[bi, h] + upd
                st_ref[bi, h] = st
                q = p_ref[bi, rows, ks]
                o = lax.dot_general(q, st.astype(BF16), (((1,), (1,)), ((), ())), preferred_element_type=F32)
                o = o * lax.rsqrt(jnp.mean(o * o, axis=-1, keepdims=True) + EPS) * gon_ref[...]
                g0 = 2 * dk_total + dv_total + h * dv
                g = p_ref[bi, rows, g0:g0 + dv].astype(F32)
                o_ref[bi, rows, vs] = (o * _silu(g)).astype(BF16)
        return carry

    lax.fori_loop(0, tc // CHUNK, chunk, 0)


def _gla_mixer(x, g_mix, mod, w_in, w_gk1, w_gk2, b_gk, g_onorm, w_out, tiles):
    b, l, d = x.shape
    tm, tc = tiles["row"], tiles["gla"]
    rank = w_gk1.shape[1]
    dk_total = w_gk2.shape[1]
    dv_total = w_out.shape[0]
    dk, dv = dk_total // GLA_HEADS, dv_total // GLA_HEADS
    n = w_in.shape[1]
    w1 = jnp.zeros((d, LANES), BF16).at[:, :rank].set(w_gk1.astype(BF16))
    w2 = jnp.zeros((LANES, dk_total), BF16).at[:rank, :].set(w_gk2.astype(BF16))
    p, la = pl.pallas_call(
        functools.partial(_gla_proj_kernel, dk_total=dk_total, qscale=dk ** -0.5),
        out_shape=(jax.ShapeDtypeStruct((b, l, n), BF16), jax.ShapeDtypeStruct((b, l, dk_total), F32)),
        grid=(b, l // tm),
        in_specs=[_row_spec(tm, d), _const_spec((1, d)), _mod_spec(d, 1), _mod_spec(d, 0),
                  _const_spec((d, n)), _const_spec((d, LANES)), _const_spec((LANES, dk_total)),
                  _const_spec((1, dk_total))],
        out_specs=(_row_spec(tm, n), _row_spec(tm, dk_total)),
        compiler_params=_params(("parallel", "parallel")),
        name="gla_proj",
    )(x, g_mix.reshape(1, d), mod, mod, w_in.astype(BF16), w1, w2, b_gk.reshape(1, dk_total))
    nb = tiles["gla_batch"] if b % tiles["gla_batch"] == 0 else 1

    def scan_spec(width):
        return pl.BlockSpec((nb, tc, width), lambda bi, ti: (bi, ti, 0))

    o = pl.pallas_call(
        functools.partial(_gla_scan_kernel, nb=nb, tc=tc, dk=dk, dv=dv),
        out_shape=jax.ShapeDtypeStruct((b, l, dv_total), BF16),
        grid=(b // nb, l // tc),
        in_specs=[scan_spec(n), scan_spec(dk_total), _const_spec((1, dv))],
        out_specs=scan_spec(dv_total),
        scratch_shapes=[pltpu.VMEM((nb, GLA_HEADS, dv, dk), F32)],
        compiler_params=_params(("arbitrary", "arbitrary")),
        name="gla_scan",
    )(p, la, g_onorm.reshape(1, dv))
    return _proj_res(_proj_res_kernel, o, w_out.astype(BF16), x, mod, 2, tm, "gla_out")


def _mlstm_kernel(xz_ref, cw_ref, cb_ref, wq_ref, wk_ref, wv_ref, wif_ref, bif_ref, skip_ref, gn_ref,
                  o_ref, ext_ref, q_ref, k_ref, v_ref, xc_ref, c_ref, n_ref, m_ref, *, tc, inner):
    heads = MLSTM_HEADS
    dh = inner // heads
    conv_w = cw_ref.shape[0]
    halo = SUBLANES

    @pl.when(pl.program_id(1) == 0)
    def _():
        ext_ref[0:halo, :] = jnp.zeros((halo, inner), F32)
        c_ref[...] = jnp.zeros_like(c_ref)
        n_ref[...] = jnp.zeros_like(n_ref)
        m_ref[...] = jnp.zeros_like(m_ref)

    ext_ref[halo:halo + tc, :] = xz_ref[:, :inner].astype(F32)
    acc = jnp.broadcast_to(cb_ref[...], (tc, inner))
    for j in range(conv_w):
        off = halo - (conv_w - 1) + j
        acc = acc + cw_ref[j:j + 1, :] * ext_ref[off:off + tc, :]
    ext_ref[0:halo, :] = ext_ref[tc:tc + halo, :]
    xc = _silu(acc)
    xc_ref[...] = xc

    gates = jnp.broadcast_to(bif_ref[...], (tc, LANES))
    for i in range(inner // MXU):
        sl = slice(i * MXU, (i + 1) * MXU)
        xcb = xc[:, sl].astype(BF16)
        q = jnp.dot(xcb, wq_ref[i], preferred_element_type=F32)
        k = jnp.dot(xcb, wk_ref[i], preferred_element_type=F32)
        v = jnp.dot(xz_ref[:, sl], wv_ref[i], preferred_element_type=F32)
        qb, kb, vb = q.astype(BF16), k.astype(BF16), v.astype(BF16)
        q_ref[:, sl] = qb
        k_ref[:, sl] = (k * dh ** -0.5).astype(BF16)
        v_ref[:, sl] = vb
        gates = gates + jnp.dot(qb, wif_ref[i * MXU:(i + 1) * MXU, :], preferred_element_type=F32)
        gates = gates + jnp.dot(kb, wif_ref[inner + i * MXU:inner + (i + 1) * MXU, :], preferred_element_type=F32)
        gates = gates + jnp.dot(vb, wif_ref[2 * inner + i * MXU:2 * inner + (i + 1) * MXU, :],
                                preferred_element_type=F32)
    rr = lax.broadcasted_iota(jnp.int32, (tc, tc), 0)
    cc = lax.broadcasted_iota(jnp.int32, (tc, tc), 1)
    cum = jnp.dot((rr >= cc).astype(F32), _log_sigmoid(gates), preferred_element_type=F32, precision=HIGHEST)
    a_all = gates - pltpu.roll(cum, LANES - heads, 1)
    a_rows = a_all.T
    chunk_r = lax.broadcasted_iota(jnp.int32, (tc, 1), 0) // CHUNK
    chunk_c = lax.broadcasted_iota(jnp.int32, (1, tc), 1) // CHUNK
    visible = chunk_c <= chunk_r
    n_chunks = tc // CHUNK
    for h in range(heads):
        hs = slice(h * dh, (h + 1) * dh)
        a_col = a_all[:, h:h + 1]
        f_col = cum[:, heads + h:heads + h + 1]
        m_in = m_ref[h][:, 0:1]
        mu = m_in
        mu_col = jnp.zeros((tc, 1), F32)
        f_end = jnp.zeros((tc, 1), F32)
        for c in range(n_chunks):
            mu = jnp.maximum(mu, jnp.max(jnp.where(chunk_r == c, a_col, NEG_BIG), axis=0, keepdims=True))
            mu_col = jnp.where(chunk_r == c, mu, mu_col)
            f_end = jnp.where(chunk_r == c, f_col[(c + 1) * CHUNK - 1:(c + 1) * CHUNK, :], f_end)
        q = q_ref[:, hs]
        k = k_ref[:, hs]
        v = v_ref[:, hs]
        w = jnp.where(visible, jnp.exp(a_rows[h:h + 1, :] - mu_col), 0.0)
        p = lax.dot_general(q, k, (((1,), (1,)), ((), ())), preferred_element_type=F32) * w
        carry_in = jnp.exp(m_in - mu_col)
        num = jnp.dot(p.astype(BF16), v, preferred_element_type=F32) \
            + carry_in * jnp.dot(q, c_ref[h].astype(BF16), preferred_element_type=F32)
        qn = carry_in * jnp.sum(q.astype(F32) * n_ref[h], axis=-1, keepdims=True) + jnp.sum(p, axis=-1, keepdims=True)
        den = jnp.maximum(jnp.abs(qn), jnp.exp(-(mu_col + f_end)))
        hc = num / den
        mean = jnp.mean(hc, axis=-1, keepdims=True)
        var = jnp.mean((hc - mean) * (hc - mean), axis=-1, keepdims=True)
        hn = (hc - mean) * lax.rsqrt(var + EPS) * gn_ref[:, hs]
        z = xz_ref[:, inner + h * dh:inner + (h + 1) * dh].astype(F32)
        o_ref[:, hs] = ((hn + skip_ref[:, hs] * xc_ref[:, hs]) * _silu(z)).astype(BF16)
        decay = jnp.exp(m_in - mu)
        kw = k.astype(F32) * jnp.exp(a_col - mu)
        c_ref[h] = decay * c_ref[h] + lax.dot_general(kw.astype(BF16), v, (((0,), (0,)), ((), ())),
                                                      preferred_element_type=F32)
        n_ref[h] = decay * n_ref[h] + jnp.sum(kw, axis=0, keepdims=True)
        m_ref[h] = jnp.broadcast_to(mu + f_col[tc - 1:tc, :], (1, LANES))


def _diag_tiles(w):
    nblk, bs, _ = w.shape
    per = MXU // bs
    wt = w.reshape(nblk // per, per, bs, bs)
    eye = jnp.eye(per, dtype=w.dtype)
    return jnp.einsum("tpio,pq->tpiqo", wt, eye).reshape(nblk // per, MXU, MXU).astype(BF16)


def _mlstm_mixer(x, g_mix, mod, w_up, conv_w, conv_b, w_q, w_k, w_v, w_if, b_if, skip, g_norm, w_down, tiles):
    b, l, d = x.shape
    tm, tc = tiles["row"], tiles["mlstm"]
    inner = w_down.shape[0]
    heads = MLSTM_HEADS
    xz = _norm_proj(x, g_mix, mod, 1, 0, w_up.astype(BF16), BF16, tm)
    wif = jnp.zeros((3 * inner, LANES), BF16).at[:, :2 * heads].set(w_if.astype(BF16))
    bif = jnp.zeros((1, LANES), F32).at[0, :2 * heads].set(b_if)
    nt = inner // MXU
    y = pl.pallas_call(
        functools.partial(_mlstm_kernel, tc=tc, inner=inner),
        out_shape=jax.ShapeDtypeStruct((b, l, inner), BF16),
        grid=(b, l // tc),
        in_specs=[_row_spec(tc, 2 * inner), _const_spec(conv_w.shape), _const_spec((1, inner)),
                  _const_spec((nt, MXU, MXU)), _const_spec((nt, MXU, MXU)), _const_spec((nt, MXU, MXU)),
                  _const_spec((3 * inner, LANES)), _const_spec((1, LANES)), _const_spec((1, inner)),
                  _const_spec((1, inner))],
        out_specs=_row_spec(tc, inner),
        scratch_shapes=[pltpu.VMEM((tc + 2 * SUBLANES, inner), F32),
                        pltpu.VMEM((tc, inner), BF16), pltpu.VMEM((tc, inner), BF16),
                        pltpu.VMEM((tc, inner), BF16), pltpu.VMEM((tc, inner), F32),
                        pltpu.VMEM((heads, inner // heads, inner // heads), F32),
                        pltpu.VMEM((heads, 1, inner // heads), F32),
                        pltpu.VMEM((heads, 1, LANES), F32)],
        compiler_params=_params(("arbitrary", "arbitrary")),
        name="mlstm_scan",
    )(xz, conv_w, conv_b.reshape(1, inner), _diag_tiles(w_q), _diag_tiles(w_k), _diag_tiles(w_v), wif, bif,
      skip.reshape(1, inner), g_norm.reshape(1, inner))
    return _proj_res(_proj_res_kernel, y, w_down.astype(BF16), x, mod, 2, tm, "mlstm_out")


def _s5_kernel(u_ref, bt_ref, ct_ref, are_ref, aim_ref, d_ref, y_ref, bu_ref, st_ref, *, ts, nb, d, npart, per):
    rows = ts * nb
    half = per // 2
    ntile = npart // MXU

    @pl.when(pl.program_id(0) == 0)
    def _():
        st_ref[...] = jnp.zeros_like(st_ref)

    def col_tile(t, j):
        return half * t + j if j < half else ntile + half * t + (j - half)

    u = u_ref[...].reshape(rows, d)
    ub = u.astype(BF16)
    for kt in range(d // MXU):
        lhs = ub[:, kt * MXU:(kt + 1) * MXU]
        for j in range(per):
            n = col_tile(kt, j)
            bu_ref[:, n * MXU:(n + 1) * MXU] = jnp.dot(lhs, bt_ref[kt * per + j], preferred_element_type=F32)

    cb = min(512, npart)
    for c in range(npart // cb):
        re = slice(c * cb, (c + 1) * cb)
        im = slice(npart + c * cb, npart + (c + 1) * cb)
        ar = are_ref[:, re]
        ai = aim_ref[:, re]

        def step(t, carry):
            xr, xi = carry
            r = pl.ds(pl.multiple_of(t * nb, nb), nb)
            nr = ar * xr - ai * xi + bu_ref[r, re]
            ni = ar * xi + ai * xr + bu_ref[r, im]
            bu_ref[r, re] = nr
            bu_ref[r, im] = ni
            return nr, ni

        xr, xi = lax.fori_loop(0, ts, step, (st_ref[:, re], st_ref[:, im]), unroll=4)
        st_ref[:, re] = xr
        st_ref[:, im] = xi

    for nt in range(d // MXU):
        acc = jnp.zeros((rows, MXU), F32)
        for j in range(per):
            n = col_tile(nt, j)
            acc = acc + jnp.dot(bu_ref[:, n * MXU:(n + 1) * MXU].astype(BF16), ct_ref[nt * per + j],
                                preferred_element_type=F32)
        ch = slice(nt * MXU, (nt + 1) * MXU)
        yv = acc + d_ref[:, ch] * u[:, ch]
        y_ref[:, :, ch] = jax.nn.gelu(yv).reshape(ts, nb, MXU)


def _s5_mixer(xs, g_mix, mods, w_in, a_re, a_im, log_dt, b_re, b_im, c_re, c_im, d_skip, w_out, tiles):
    l, d = xs[0].shape[1:]
    b = sum(p.shape[0] for p in xs)
    tm, ts = tiles["row"], tiles["s5"]
    groups, pst = a_re.shape
    npart = groups * pst
    gk = MXU // S5_GROUP
    gn = MXU // pst
    per = 2 * (gk // gn)
    nkt = d // MXU

    a = lax.complex(a_re, a_im)
    dt = jnp.exp(log_dt)[:, None]
    a_bar = jnp.exp(a * dt)
    b_bar = ((a_bar - 1.0) / a)[..., None] * lax.complex(b_re, b_im)
    eye = jnp.eye(groups, dtype=F32)
    bfull = jnp.concatenate([jnp.einsum("gpc,gh->gchp", part, eye).reshape(d, npart)
                             for part in (jnp.real(b_bar), jnp.imag(b_bar))], axis=1)
    cfull = jnp.concatenate([jnp.einsum("gcp,gh->gphc", part, eye).reshape(npart, d)
                             for part in (c_re, -c_im)], axis=0)
    half = per // 2
    ntile = npart // MXU

    def col_tile(t, j):
        return half * t + j if j < half else ntile + half * t + (j - half)

    bt = jnp.stack([bfull[kt * MXU:(kt + 1) * MXU, col_tile(kt, j) * MXU:(col_tile(kt, j) + 1) * MXU]
                    for kt in range(nkt) for j in range(per)]).astype(BF16)
    ct = jnp.stack([cfull[col_tile(nt, j) * MXU:(col_tile(nt, j) + 1) * MXU, nt * MXU:(nt + 1) * MXU]
                    for nt in range(nkt) for j in range(per)]).astype(BF16)
    are = jnp.broadcast_to(jnp.real(a_bar).reshape(1, npart), (b, npart))
    aim = jnp.broadcast_to(jnp.imag(a_bar).reshape(1, npart), (b, npart))

    win = w_in.astype(BF16)
    ut, b0 = None, 0
    for x, mod in zip(xs, mods):
        ut = _norm_proj_time_major(x, g_mix, mod, 1, 0, win, tm, ut, b0, b)
        b0 += x.shape[0]
    ut = ut.reshape(l, b, d)
    yt = pl.pallas_call(
        functools.partial(_s5_kernel, ts=ts, nb=b, d=d, npart=npart, per=per),
        out_shape=jax.ShapeDtypeStruct((l, b, d), F32),
        grid=(l // ts,),
        in_specs=[pl.BlockSpec((ts, b, d), lambda t: (t, 0, 0)),
                  pl.BlockSpec((nkt * per, MXU, MXU), lambda t: (0, 0, 0)),
                  pl.BlockSpec((nkt * per, MXU, MXU), lambda t: (0, 0, 0)),
                  pl.BlockSpec((b, npart), lambda t: (0, 0)),
                  pl.BlockSpec((b, npart), lambda t: (0, 0)),
                  pl.BlockSpec((1, d), lambda t: (0, 0))],
        out_specs=pl.BlockSpec((ts, b, d), lambda t: (t, 0, 0)),
        scratch_shapes=[pltpu.VMEM((ts * b, 2 * npart), F32), pltpu.VMEM((b, 2 * npart), F32)],
        compiler_params=_params(("arbitrary",)),
        name="s5_scan",
    )(ut, bt, ct, are, aim, d_skip.reshape(1, d))
    yt = yt.reshape(l, b * d)
    wout = w_out.astype(BF16)
    outs, b0 = [], 0
    for x, mod in zip(xs, mods):
        outs.append(_proj_res(_glu_res_kernel, yt, wout, x, mod, 2, tm, "s5_out", y_boff=b0))
        b0 += x.shape[0]
    return outs


def _router_kernel(x_ref, g_ref, sc_ref, sh_ref, wr_ref, br_ref, h_ref, ri_ref, rw_ref, cnt_ref, run_ref, *, tm):
    @pl.when((pl.program_id(0) == 0) & (pl.program_id(1) == 0))
    def _():
        run_ref[...] = jnp.zeros_like(run_ref)

    hb = _normmod(x_ref[...], g_ref[...], sc_ref[...], sh_ref[...]).astype(BF16)
    h_ref[...] = hb
    logits = lax.dot_general(wr_ref[...], hb, (((1,), (1,)), ((), ())), preferred_element_type=F32) + br_ref[...]
    n_exp = logits.shape[0]
    erow = lax.broadcasted_iota(jnp.int32, logits.shape, 0).astype(F32)
    work = logits
    vals, idxs = [], []
    for _ in range(TOP_K):
        m = jnp.max(work, axis=0, keepdims=True)
        idx = jnp.min(jnp.where(work == m, erow, float(n_exp)), axis=0, keepdims=True)
        vals.append(m)
        idxs.append(idx)
        work = jnp.where(erow == idx, NEG_BIG, work)
    exps = [jnp.exp(v - vals[0]) for v in vals]
    denom = exps[0]
    for e in exps[1:]:
        denom = denom + e
    multi = jnp.zeros(logits.shape, F32)
    for idx in idxs:
        multi = multi + (erow == idx).astype(F32)
    lane = lax.broadcasted_iota(jnp.int32, logits.shape, 1)
    inc = multi
    shift = 1
    while shift < tm:
        inc = inc + jnp.where(lane >= shift, pltpu.roll(inc, shift, 1), 0.0)
        shift *= 2
    run = run_ref[:, 0:1]
    cs = inc - multi + run
    run = run + inc[:, tm - 1:tm]
    run_ref[...] = jnp.broadcast_to(run, run_ref.shape)
    cnt_ref[...] = jnp.broadcast_to(run, cnt_ref.shape).astype(jnp.int32)
    srow = lax.broadcasted_iota(jnp.int32, (2 * TOP_K, tm), 0)
    ri = jnp.zeros((2 * TOP_K, tm), F32)
    rw = jnp.zeros((2 * TOP_K, tm), F32)
    for k in range(TOP_K):
        rank = jnp.sum(jnp.where(erow == idxs[k], cs, 0.0), axis=0, keepdims=True)
        ri = ri + jnp.where(srow == k, idxs[k], 0.0) + jnp.where(srow == TOP_K + k, rank, 0.0)
        rw = rw + jnp.where(srow == k, exps[k] / denom, 0.0)
    ri_ref[...] = ri.astype(jnp.int32)
    rw_ref[...] = jnp.concatenate([rw, jnp.zeros((LANES - 2 * TOP_K, tm), F32)], axis=0).T


def _dest_kernel(offs_ref, ri_ref, o_ref, *, n_exp):
    idx = ri_ref[0:TOP_K, :]
    off = jnp.zeros(idx.shape, jnp.int32)
    for e in range(n_exp):
        off = jnp.where(idx == e, offs_ref[e], off)
    o_ref[0:TOP_K, :] = off + ri_ref[TOP_K:2 * TOP_K, :]
    o_ref[TOP_K:2 * TOP_K, :] = idx


def _expert_kernel(te_ref, nv_ref, xs_ref, wgu_ref, bgu_ref, wd_ref, bd_ref, o_ref, *rest, dff, cast):
    i = pl.program_id(0)
    valid = i < nv_ref[0]
    if cast:
        wgu_bf, wd_bf, act_ref = rest
        cast_rows = 2 * SUBLANES * SUBLANES

        @pl.when(valid & ((i == 0) | (te_ref[i] != te_ref[jnp.maximum(i - 1, 0)])))
        def _():
            def cast_gu(r, carry):
                rows = pl.ds(pl.multiple_of(r * cast_rows, cast_rows), cast_rows)
                wgu_bf[rows, :] = wgu_ref[rows, :].astype(BF16)
                return carry

            def cast_d(r, carry):
                rows = pl.ds(pl.multiple_of(r * cast_rows, cast_rows), cast_rows)
                wd_bf[rows, :] = wd_ref[rows, :].astype(BF16)
                return carry

            lax.fori_loop(0, wgu_ref.shape[0] // cast_rows, cast_gu, 0)
            lax.fori_loop(0, wd_ref.shape[0] // cast_rows, cast_d, 0)
    else:
        wgu_bf, wd_bf = wgu_ref, wd_ref
        (act_ref,) = rest

    @pl.when(valid)
    def _():
        x = xs_ref[...]
        gate = jnp.dot(x, wgu_bf[:, :dff], preferred_element_type=F32) + bgu_ref[:, :dff]
        up = jnp.dot(x, wgu_bf[:, dff:], preferred_element_type=F32) + bgu_ref[:, dff:]
        gate = jnp.minimum(gate, SWIGLU_LIMIT)
        up = jnp.clip(up, -SWIGLU_LIMIT, SWIGLU_LIMIT)
        act_ref[...] = ((up + 1.0) * gate * jax.nn.sigmoid(SWIGLU_ALPHA * gate)).astype(BF16)
        o_ref[...] = (jnp.dot(act_ref[...], wd_bf[...], preferred_element_type=F32) + bd_ref[...]).astype(o_ref.dtype)

    @pl.when(jnp.logical_not(valid))
    def _():
        o_ref[...] = jnp.zeros_like(o_ref)


def _combine_kernel(y0_ref, y1_ref, y2_ref, y3_ref, rw_ref, x_ref, gt_ref, gf_ref, *rest, final):
    o_ref = rest[-1]
    acc = jnp.zeros(o_ref.shape, F32)
    for k, y_ref in enumerate((y0_ref, y1_ref, y2_ref, y3_ref)):
        acc = acc + rw_ref[:, k:k + 1] * y_ref[...].astype(F32)
    out = x_ref[...] + gt_ref[...] * acc
    if final:
        out = out * lax.rsqrt(jnp.mean(out * out, axis=-1, keepdims=True) + EPS) * gf_ref[...]
    o_ref[...] = out


def _moe_route(x, g_ffn, mod, w_router, b_router, tiles):
    b, l, d = x.shape
    t = b * l
    tm, te = tiles["row"], tiles["expert"]
    n_exp = w_router.shape[1]
    assert TOP_K == 4
    wr = w_router.T.astype(BF16)
    br = b_router.reshape(n_exp, 1)
    nt = l // tm
    h2, ri, rw, cnt = pl.pallas_call(
        functools.partial(_router_kernel, tm=tm),
        out_shape=(jax.ShapeDtypeStruct((b, l, d), BF16), jax.ShapeDtypeStruct((2 * TOP_K, t), jnp.int32),
                   jax.ShapeDtypeStruct((b, l, LANES), F32), jax.ShapeDtypeStruct((n_exp, LANES), jnp.int32)),
        grid=(b, nt),
        in_specs=[_row_spec(tm, d), _const_spec((1, d)), _mod_spec(d, 4), _mod_spec(d, 3),
                  _const_spec((n_exp, d)), _const_spec((n_exp, 1))],
        out_specs=(_row_spec(tm, d), pl.BlockSpec((2 * TOP_K, tm), lambda bi, ti: (0, bi * nt + ti)),
                   _row_spec(tm, LANES), _const_spec((n_exp, LANES))),
        scratch_shapes=[pltpu.VMEM((n_exp, LANES), F32)],
        compiler_params=_params(("arbitrary", "arbitrary")),
        name="moe_router",
    )(x, g_ffn.reshape(1, d), mod, mod, wr, br)

    counts = cnt[:, 0]
    padded = jnp.maximum((counts + te - 1) // te, 1) * te
    ends = jnp.cumsum(padded)
    offs = ends - padded
    n_rows = t * TOP_K + n_exp * te
    n_tiles = n_rows // te
    tcol = min(t, 8192)
    dest = pl.pallas_call(
        functools.partial(_dest_kernel, n_exp=n_exp),
        out_shape=jax.ShapeDtypeStruct((2 * TOP_K, t), jnp.int32),
        grid_spec=pltpu.PrefetchScalarGridSpec(
            num_scalar_prefetch=1,
            grid=(t // tcol,),
            in_specs=[pl.BlockSpec((2 * TOP_K, tcol), lambda i, o: (0, i))],
            out_specs=pl.BlockSpec((2 * TOP_K, tcol), lambda i, o: (0, i))),
        compiler_params=_params(("parallel",)),
        name="moe_dest",
    )(offs.astype(jnp.int32), ri)[:TOP_K]
    slot = jnp.arange(te, dtype=jnp.int32)[None, :]
    n_pad = (padded - counts)[:, None]
    n_tail = te - n_pad
    tail_base = ends[-1] + (jnp.cumsum(n_tail[:, 0]) - n_tail[:, 0])[:, None]
    free_rows = jnp.where(slot < n_pad, (offs + counts)[:, None] + slot, tail_base + slot - n_pad)
    filler = ((jnp.arange(n_exp * te, dtype=jnp.int32) * 61) % t).reshape(n_exp, te)
    filler = jnp.where((slot == 0) & (counts[:, None] % 2 == 1), t - 1, filler)
    keys = jnp.concatenate([dest.reshape(-1), free_rows.reshape(-1).astype(jnp.int32)])
    vals = jnp.concatenate([jnp.tile(jnp.arange(t, dtype=jnp.int32), TOP_K), filler.reshape(-1)])
    tok_bits = (t - 1).bit_length()
    assert (((n_rows // 2 - 1) << tok_bits) | (t - 1)) < 2 ** 32 and te % 2 == 0
    packed = ((keys >> 1).astype(jnp.uint32) << tok_bits) | vals.astype(jnp.uint32)
    src = (jnp.sort(packed) & jnp.uint32((1 << tok_bits) - 1)).astype(jnp.int32)
    tile_start = jnp.arange(n_tiles, dtype=jnp.int32) * te
    tile_exp = jnp.minimum(jnp.sum(tile_start[:, None] >= ends[None, :], axis=1), n_exp - 1).astype(jnp.int32)
    n_valid = (ends[-1:] // te).astype(jnp.int32)
    return dict(h2=h2, src=src, dest=dest, rw=rw, tile_exp=tile_exp, n_valid=n_valid)


def _moe_dispatch(route):
    h2 = route["h2"]
    return h2.reshape(-1, h2.shape[-1]).at[route["src"]].get(mode="promise_in_bounds")


def _moe_experts(route, xs, w_gu, b_gu, w_down, b_down, tiles, layer=None):
    n_rows, d = xs.shape
    te = tiles["expert"]
    cast = layer is not None
    n_exp, dff = w_down.shape[-3:-1]
    n_tiles = n_rows // te
    if cast:
        def wspec(r, c):
            return pl.BlockSpec((None, None, r, c), lambda i, e, nv: (layer, e[i], 0, 0))
        b_gu, b_down = b_gu[layer], b_down[layer]
    else:
        def wspec(r, c):
            return pl.BlockSpec((None, r, c), lambda i, e, nv: (e[i], 0, 0))

    def espec(r, c):
        return pl.BlockSpec((None, r, c), lambda i, e, nv: (e[i], 0, 0))

    rows_spec = pl.BlockSpec((te, d), lambda i, e, nv: (i, 0))
    out_shape = [jax.ShapeDtypeStruct((n_rows, d), BF16)]
    out_specs = [rows_spec]
    if cast:
        out_shape += [jax.ShapeDtypeStruct((n_exp, d, 2 * dff), BF16), jax.ShapeDtypeStruct((n_exp, dff, d), BF16)]
        out_specs += [espec(d, 2 * dff), espec(dff, d)]
    res = pl.pallas_call(
        functools.partial(_expert_kernel, dff=dff, cast=cast),
        out_shape=tuple(out_shape),
        grid_spec=pltpu.PrefetchScalarGridSpec(
            num_scalar_prefetch=2,
            grid=(n_tiles,),
            in_specs=[rows_spec, wspec(d, 2 * dff), espec(1, 2 * dff), wspec(dff, d), espec(1, d)],
            out_specs=tuple(out_specs),
            scratch_shapes=[pltpu.VMEM((te, dff), BF16)]),
        compiler_params=_params(("arbitrary",)),
        name="moe_experts_cast" if cast else "moe_experts",
    )(route["tile_exp"], route["n_valid"], xs, w_gu, b_gu.reshape(n_exp, 1, 2 * dff), w_down,
      b_down.reshape(n_exp, 1, d))
    return res if cast else res[0]


def _moe_gather_back(route, ys):
    return [ys.at[route["dest"][k]].get(mode="promise_in_bounds") for k in range(TOP_K)]


def _moe_combine(x, yk, route, mod, g_final, final, tiles, out=None):
    b, l, d = x.shape
    t = b * l
    tm = tiles["row"]
    yk = [y.reshape(b, l, d) for y in yk]
    rw = route["rw"]
    in_specs = [_row_spec(tm, d)] * TOP_K + [_row_spec(tm, LANES), _row_spec(tm, d), _mod_spec(d, 5),
                                             _const_spec((1, d))]
    args = [*yk, rw, x, mod, g_final.reshape(1, d)]
    buf, boff, b_total = (None, 0, b) if out is None else out
    aliases = {}
    if buf is not None:
        in_specs.append(pl.BlockSpec(memory_space=pl.ANY))
        args.append(buf)
        aliases = {len(args) - 1: 0}
    return pl.pallas_call(
        functools.partial(_combine_kernel, final=final),
        out_shape=jax.ShapeDtypeStruct((b_total, l, d), F32),
        grid=(b, l // tm),
        in_specs=in_specs,
        out_specs=pl.BlockSpec((None, tm, d), lambda bi, ti: (boff + bi, ti, 0)),
        input_output_aliases=aliases,
        compiler_params=_params(("parallel", "parallel")),
        name="moe_combine",
    )(*args)


def kernel(x, c, g_mix, g_ffn, w_ada, b_ada, gla_w_in, gla_w_gk1, gla_w_gk2, gla_b_gk, gla_g_onorm, gla_w_out, ml_w_up, ml_conv_w, ml_conv_b, ml_w_q, ml_w_k, ml_w_v, ml_w_if, ml_b_if, ml_skip, ml_g_norm, ml_w_down, s5_w_in, s5_a_re, s5_a_im, s5_log_dt, s5_b_re, s5_b_im, s5_c_re, s5_c_im, s5_d, s5_w_out, moe_w_router, moe_b_router, moe_w_gu, moe_b_gu, moe_w_down, moe_b_down, g_final):
    b, l, d = x.shape
    depth = w_ada.shape[0]
    n_exp = moe_w_router.shape[-1]
    assert l % CHUNK == 0 and d % MXU == 0
    n_parts = 2 if b % 2 == 0 else 1
    bp = b // n_parts
    tiles = _tiles(l, bp * l, n_exp)
    mods = _modulation(c, w_ada, b_ada)
    xs = [x[s * bp:(s + 1) * bp] for s in range(n_parts)]
    after = lax.optimization_barrier
    for i in range(depth):
        mp = [mods[i, s * bp:(s + 1) * bp].reshape(bp, 6, 1, d) for s in range(n_parts)]
        kind, j = i % N_MIXERS, i // N_MIXERS
        last = i == depth - 1

        def mixer(xp, mod):
            if kind == 0:
                return _gla_mixer(xp, g_mix[i], mod, gla_w_in[j], gla_w_gk1[j], gla_w_gk2[j], gla_b_gk[j],
                                  gla_g_onorm[j], gla_w_out[j], tiles)
            return _mlstm_mixer(xp, g_mix[i], mod, ml_w_up[j], ml_conv_w[j], ml_conv_b[j], ml_w_q[j], ml_w_k[j],
                                ml_w_v[j], ml_w_if[j], ml_b_if[j], ml_skip[j], ml_g_norm[j], ml_w_down[j], tiles)

        def route(xp, mod):
            return _moe_route(xp, g_ffn[i], mod, moe_w_router[i], moe_b_router[i], tiles)

        def experts_casting(r, rows):
            return _moe_experts(r, rows, moe_w_gu, moe_b_gu, moe_w_down, moe_b_down, tiles, layer=i)

        joint = kind == 2
        if joint:
            xs = _s5_mixer(xs, g_mix[i], mp, s5_w_in[j], s5_a_re[j], s5_a_im[j], s5_log_dt[j], s5_b_re[j],
                           s5_b_im[j], s5_c_re[j], s5_c_im[j], s5_d[j], s5_w_out[j], tiles)
        if n_parts == 1:
            x0 = xs[0] if joint else mixer(xs[0], mp[0])
            r0 = route(x0, mp[0])
            y0 = _moe_gather_back(r0, experts_casting(r0, _moe_dispatch(r0))[0])
            xs = [_moe_combine(x0, y0, r0, mp[0], g_final, last, tiles)]
            continue
        xa, xb = xs
        xa = xa if joint else mixer(xa, mp[0])
        ra = route(xa, mp[0])
        ra["src"], xb = after((ra["src"], xb))
        rows_a = _moe_dispatch(ra)
        xb = xb if joint else mixer(xb, mp[1])
        rb = route(xb, mp[1])
        rows_a, rb["src"] = after((rows_a, rb["src"]))
        rows_b = _moe_dispatch(rb)
        ys_a, wgu, wdn = experts_casting(ra, rows_a)
        back_a = _moe_gather_back(ra, ys_a)
        ys_b = _moe_experts(rb, rows_b, wgu, moe_b_gu[i], wdn, moe_b_down[i], tiles)
        back_a, ys_b = after((back_a, ys_b))
        back_b = _moe_gather_back(rb, ys_b)
        xa = _moe_combine(xa, back_a, ra, mp[0], g_final, last, tiles, (None, 0, b) if last else None)
        xb = _moe_combine(xb, back_b, rb, mp[1], g_final, last, tiles, (xa, bp, b) if last else None)
        xs = [xa, xb]
    return xs[-1] if n_parts == 2 else xs[0]
```
